```python
import jax, jax.numpy as jnp
from jax import lax
import numpy as np

D_MODEL = 4096
BATCH = 1
SEQ = 16384
DEPTH = 4

POOL_GROUPS = 4
POOL_WIDTH = D_MODEL // 4
POOL_GROUP_DIM = POOL_WIDTH // POOL_GROUPS
POOL_WINDOWS = (2, 4, 8, 16)
N_HEADS = 16
HEAD_DIM = 64
ATTN_WIDTH = N_HEADS * HEAD_DIM
GRID_W = 64
WIN_ROWS_MAX = 8
WIN_COLS = 16
IN_WIDTH = POOL_WIDTH + 3 * ATTN_WIDTH
N_BRANCHES = 2
D_FF = 4 * D_MODEL
PLE_DIM = 256
PLE_GATE_RANK = 256
EPS = 1e-6

kernel_name = 'hybrid_pool_natten_encoder'


def _rmsnorm(x, g):
    xf = x.astype(jnp.float32)
    y = xf * lax.rsqrt(jnp.mean(xf * xf, axis=-1, keepdims=True) + EPS)
    return (y * g.astype(jnp.float32)).astype(x.dtype)


def _pool_mixer(u, w_pool, pool_scale):
    B, S, _ = u.shape
    t = jnp.arange(S)
    uf = u.astype(jnp.float32)
    csum = jnp.concatenate([jnp.zeros((B, 1, POOL_WIDTH), jnp.float32),
                            jnp.cumsum(uf, axis=1)], axis=1)
    outs = []
    for g, w in enumerate(POOL_WINDOWS):
        lo_c, hi_c = g * POOL_GROUP_DIM, (g + 1) * POOL_GROUP_DIM
        lo = jnp.clip(t - w // 2, 0, S)
        hi = jnp.clip(t + w // 2, 0, S)
        cg = csum[:, :, lo_c:hi_c]
        cnt = (hi - lo).astype(jnp.float32)[None, :, None]
        mean = (jnp.take(cg, hi, axis=1) - jnp.take(cg, lo, axis=1)) / cnt
        d = (mean - uf[:, :, lo_c:hi_c]).astype(u.dtype)
        outs.append(d @ w_pool[g])
    return jnp.concatenate(outs, axis=-1) * pool_scale


def _neighbourhood_attention(q, k, v, q_norm, k_norm, rpb):
    B, S, _ = q.shape
    rows = S // GRID_W
    kr = min(WIN_ROWS_MAX, rows)
    grid = (B, rows, GRID_W, N_HEADS, HEAD_DIM)
    qg = _rmsnorm(q.reshape(grid), q_norm)
    kg = _rmsnorm(k.reshape(grid), k_norm)
    vg = v.reshape(grid)
    cols = jnp.arange(GRID_W)
    c0 = jnp.clip(cols - WIN_COLS // 2, 0, GRID_W - WIN_COLS)
    col_idx = c0[:, None] + jnp.arange(WIN_COLS)[None, :]
    dc_idx = col_idx - cols[:, None] + (WIN_COLS - 1)
    rpb_c = rpb[:, :, dc_idx]
    scale = HEAD_DIM ** -0.5

    def row_block(r):
        r0 = jnp.clip(r - kr // 2, 0, rows - kr)
        q_r = lax.dynamic_index_in_dim(qg, r, axis=1, keepdims=False)
        k_r = lax.dynamic_slice_in_dim(kg, r0, kr, axis=1)
        v_r = lax.dynamic_slice_in_dim(vg, r0, kr, axis=1)
        k_w = k_r[:, :, col_idx]
        v_w = v_r[:, :, col_idx]
        s = jnp.einsum('bchd,bicjhd->bhcij', q_r, k_w,
                       preferred_element_type=jnp.float32) * scale
        dr_idx = r0 + jnp.arange(kr) - r + (WIN_ROWS_MAX - 1)
        bias = jnp.transpose(rpb_c[:, dr_idx], (0, 2, 1, 3))
        s = s + bias[None].astype(jnp.float32)
        pr = jax.nn.softmax(s.reshape(B, N_HEADS, GRID_W, kr * WIN_COLS), axis=-1)
        pr = pr.reshape(s.shape).astype(v.dtype)
        return jnp.einsum('bhcij,bicjhd->bchd', pr, v_w)

    o = lax.map(row_block, jnp.arange(rows))
    return jnp.transpose(o, (1, 0, 2, 3, 4)).reshape(B, S, ATTN_WIDTH)


def setup_inputs(seed: int = 0) -> dict:
    key = jax.random.key(seed)
    ks = jax.random.split(key, 20)
    L, D = DEPTH, D_MODEL

    def w(k, shape, fan_in):
        return jax.random.normal(k, shape, jnp.float32) * (fan_in ** -0.5)

    def gain(k, shape):
        return 1.0 + 0.1 * jax.random.normal(k, shape, jnp.float32)

    return {
        'x': jax.random.normal(ks[0], (BATCH, SEQ, D), jnp.float32),
        'p': jax.random.normal(ks[1], (DEPTH, BATCH, SEQ, PLE_DIM), jnp.float32),
        'norm_mix': gain(ks[2], (L, D)),
        'w_in': w(ks[3], (L, D, IN_WIDTH), D),
        'w_pool': w(ks[4], (L, POOL_GROUPS, POOL_GROUP_DIM, POOL_GROUP_DIM), POOL_GROUP_DIM),
        'pool_scale': gain(ks[5], (L, POOL_WIDTH)),
        'q_norm': gain(ks[6], (L, HEAD_DIM)),
        'k_norm': gain(ks[7], (L, HEAD_DIM)),
        'rpb': 0.1 * jax.random.normal(ks[8], (L, N_HEADS, 2 * WIN_ROWS_MAX - 1, 2 * WIN_COLS - 1), jnp.float32),
        'w_branch_pool': w(ks[9], (L, POOL_WIDTH, D), POOL_WIDTH),
        'w_branch_attn': w(ks[10], (L, ATTN_WIDTH, D), ATTN_WIDTH),
        'w_gate': w(ks[11], (L, D, N_BRANCHES * D), D),
        'w_out': w(ks[12], (L, D, D), D),
        'norm_mlp': gain(ks[13], (L, D)),
        'w_up': w(ks[14], (L, D, D_FF), D),
        'w_down': w(ks[15], (L, D_FF, D), D_FF),
        'norm_ple': gain(ks[16], (L, D)),
        'w_ple_gate_down': w(ks[17], (L, D, PLE_GATE_RANK), D),
        'w_ple_gate_up': w(ks[18], (L, PLE_GATE_RANK, D), PLE_GATE_RANK),
        'w_ple_proj': w(ks[19], (L, PLE_DIM, D), PLE_DIM),
    }


def reference(x, p, norm_mix, w_in, w_pool, pool_scale, q_norm, k_norm, rpb,
              w_branch_pool, w_branch_attn, w_gate, w_out, norm_mlp, w_up, w_down,
              norm_ple, w_ple_gate_down, w_ple_gate_up, w_ple_proj):
    for i in range(DEPTH):
        h = _rmsnorm(x, norm_mix[i])
        z = h @ w_in[i]
        u = z[..., :POOL_WIDTH]
        q = z[..., POOL_WIDTH:POOL_WIDTH + ATTN_WIDTH]
        k = z[..., POOL_WIDTH + ATTN_WIDTH:POOL_WIDTH + 2 * ATTN_WIDTH]
        v = z[..., POOL_WIDTH + 2 * ATTN_WIDTH:]
        a = _pool_mixer(u, w_pool[i], pool_scale[i]) @ w_branch_pool[i]
        b = _neighbourhood_attention(q, k, v, q_norm[i], k_norm[i], rpb[i]) @ w_branch_attn[i]
        gates = jax.nn.sigmoid(h @ w_gate[i])
        merged = gates[..., :D_MODEL] * a + gates[..., D_MODEL:] * b
        x = x + merged @ w_out[i]
        h = _rmsnorm(x, norm_mlp[i])
        x = x + jnp.square(jax.nn.relu(h @ w_up[i])) @ w_down[i]
        h = _rmsnorm(x, norm_ple[i])
        g = jax.nn.sigmoid((h @ w_ple_gate_down[i]) @ w_ple_gate_up[i])
        x = x + g * (p[i] @ w_ple_proj[i])
    return x
```

```python
import functools

import jax
import jax.numpy as jnp
from jax import lax
from jax.experimental import pallas as pl
from jax.experimental.pallas import tpu as pltpu

POOL_GROUPS = 4
POOL_WIDTH = 1024
POOL_GROUP_DIM = 256
POOL_WINDOWS = (2, 4, 8, 16)
POOL_HALO = 8
N_HEADS = 16
HEAD_DIM = 64
ATTN_WIDTH = 1024
GRID_W = 64
WIN_ROWS = 8
WIN_COLS = 16
PLE_DIM = 256
EPS = 1e-6

LANES = 128
VMEM_LIMIT = 56 * 1024 * 1024
MASK_BIAS = -1e30

F32 = jnp.float32
BF16 = jnp.bfloat16


def _params(*sem):
    return pltpu.CompilerParams(dimension_semantics=sem, vmem_limit_bytes=VMEM_LIMIT)


def _sigmoid(x):
    return 1.0 / (1.0 + jnp.exp(-x))


def _rmsnorm_kernel(x_ref, g_ref, o_ref):
    x = x_ref[...]
    ms = jnp.mean(x * x, axis=-1, keepdims=True)
    o_ref[...] = (x * lax.rsqrt(ms + EPS) * g_ref[...]).astype(o_ref.dtype)


def _rmsnorm(x, g, *, tm=512):
    s, d = x.shape
    return pl.pallas_call(
        _rmsnorm_kernel,
        grid=(s // tm,),
        in_specs=[pl.BlockSpec((tm, d), lambda i: (i, 0)),
                  pl.BlockSpec((1, d), lambda i: (0, 0))],
        out_specs=pl.BlockSpec((tm, d), lambda i: (i, 0)),
        out_shape=jax.ShapeDtypeStruct((s, d), BF16),
        compiler_params=_params("parallel"),
        name="rmsnorm",
    )(x, g.reshape(1, d))


def _mm_plain_kernel(a_ref, w_ref, o_ref):
    o_ref[...] = jnp.dot(a_ref[...], w_ref[...],
                         preferred_element_type=F32).astype(o_ref.dtype)


def _mm_relu2_kernel(a_ref, w_ref, o_ref):
    y = jnp.maximum(jnp.dot(a_ref[...], w_ref[...], preferred_element_type=F32), 0.0)
    o_ref[...] = (y * y).astype(o_ref.dtype)


def _mm_residual_kernel(a_ref, w_ref, x_ref, o_ref):
    o_ref[...] = x_ref[...] + jnp.dot(a_ref[...], w_ref[...], preferred_element_type=F32)


def _head_rmsnorm(acc, g):
    lane = lax.broadcasted_iota(jnp.int32, (1, LANES), 1)
    low = lane < HEAD_DIM
    outs = []
    for c in range(acc.shape[1] // LANES):
        a = acc[:, c * LANES:(c + 1) * LANES]
        sq = a * a
        s_low = jnp.sum(jnp.where(low, sq, 0.0), axis=-1, keepdims=True)
        s_high = jnp.sum(jnp.where(low, 0.0, sq), axis=-1, keepdims=True)
        ms = jnp.where(low, s_low, s_high) * (1.0 / HEAD_DIM)
        outs.append(a * lax.rsqrt(ms + EPS) * g[:, c * LANES:(c + 1) * LANES])
    return jnp.concatenate(outs, axis=-1)


def _mm_headnorm_kernel(a_ref, w_ref, g_ref, o_ref):
    acc = jnp.dot(a_ref[...], w_ref[...], preferred_element_type=F32)
    o_ref[...] = _head_rmsnorm(acc, g_ref[...]).astype(o_ref.dtype)


def _matmul(kernel, a, w, *, n_cols, col_off=0, out_dtype, tm=1024, tn=1024,
            residual=None, gain=None, name):
    m, k = a.shape
    tm, tn = min(tm, m), min(tn, n_cols)
    off = col_off // tn
    in_specs = [pl.BlockSpec((tm, k), lambda i, j: (i, 0)),
                pl.BlockSpec((k, tn), lambda i, j: (0, j + off))]
    args = [a, w]
    if residual is not None:
        in_specs.append(pl.BlockSpec((tm, tn), lambda i, j: (i, j)))
        args.append(residual)
    if gain is not None:
        in_specs.append(pl.BlockSpec((None, 1, tn), lambda i, j: (j, 0, 0)))
        args.append(gain)
    return pl.pallas_call(
        kernel,
        grid=(m // tm, n_cols // tn),
        in_specs=in_specs,
        out_specs=pl.BlockSpec((tm, tn), lambda i, j: (i, j)),
        out_shape=jax.ShapeDtypeStruct((m, n_cols), out_dtype),
        compiler_params=_params("parallel", "parallel"),
        name=name,
    )(*args)


def _mm_acc_residual_kernel(a_ref, w_ref, x_ref, o_ref, acc_ref):
    k = pl.program_id(2)
    part = jnp.dot(a_ref[...], w_ref[...], preferred_element_type=F32)

    @pl.when(k == 0)
    def _():
        acc_ref[...] = part

    @pl.when(k > 0)
    def _():
        acc_ref[...] += part

    @pl.when(k == pl.num_programs(2) - 1)
    def _():
        o_ref[...] = x_ref[...] + acc_ref[...]


def _matmul_ksplit_residual(a, w, x, *, tm=1024, tn=1024, tk=2048, name):
    m, kdim = a.shape
    n = w.shape[1]
    tm, tn, tk = min(tm, m), min(tn, n), min(tk, kdim)
    return pl.pallas_call(
        _mm_acc_residual_kernel,
        grid=(m // tm, n // tn, kdim // tk),
        in_specs=[pl.BlockSpec((tm, tk), lambda i, j, k: (i, k)),
                  pl.BlockSpec((tk, tn), lambda i, j, k: (k, j)),
                  pl.BlockSpec((tm, tn), lambda i, j, k: (i, j))],
        out_specs=pl.BlockSpec((tm, tn), lambda i, j, k: (i, j)),
        out_shape=jax.ShapeDtypeStruct((m, n), F32),
        scratch_shapes=[pltpu.VMEM((tm, tn), F32)],
        compiler_params=_params("parallel", "parallel", "arbitrary"),
        name=name,
    )(a, w, x)


def _pool_kernel(prev_ref, u_ref, next_ref, wp_ref, ps_ref, o_ref, buf_ref, *, seq):
    i = pl.program_id(0)
    tm = u_ref.shape[0]
    buf_ref[0:POOL_HALO, :] = jnp.where(i > 0, prev_ref[...], 0.0)
    buf_ref[POOL_HALO:POOL_HALO + tm, :] = u_ref[...]
    buf_ref[POOL_HALO + tm:, :] = jnp.where(i < pl.num_programs(0) - 1, next_ref[...], 0.0)
    t = i * tm + lax.broadcasted_iota(jnp.int32, (tm, 1), 0)
    for g, w in enumerate(POOL_WINDOWS):
        cols = slice(g * POOL_GROUP_DIM, (g + 1) * POOL_GROUP_DIM)
        total = None
        for o in range(-(w // 2), w // 2):
            piece = buf_ref[POOL_HALO + o:POOL_HALO + o + tm, cols]
            total = piece if total is None else total + piece
        cnt = (jnp.minimum(t + w // 2, seq) - jnp.maximum(t - w // 2, 0)).astype(F32)
        d = total / cnt - u_ref[:, cols]
        y = jnp.dot(d.astype(BF16), wp_ref[g], preferred_element_type=F32)
        o_ref[:, cols] = (y * ps_ref[:, cols]).astype(o_ref.dtype)


def _pool_mixer(u, w_pool, pool_scale, *, tm=512):
    s, c = u.shape
    halo_blocks = tm // POOL_HALO
    return pl.pallas_call(
        functools.partial(_pool_kernel, seq=s),
        grid=(s // tm,),
        in_specs=[
            pl.BlockSpec((POOL_HALO, c), lambda i: (jnp.maximum(i * halo_blocks - 1, 0), 0)),
            pl.BlockSpec((tm, c), lambda i: (i, 0)),
            pl.BlockSpec((POOL_HALO, c),
                         lambda i: (jnp.minimum((i + 1) * halo_blocks, s // POOL_HALO - 1), 0)),
            pl.BlockSpec((POOL_GROUPS, POOL_GROUP_DIM, POOL_GROUP_DIM), lambda i: (0, 0, 0)),
            pl.BlockSpec((1, c), lambda i: (0, 0)),
        ],
        out_specs=pl.BlockSpec((tm, c), lambda i: (i, 0)),
        out_shape=jax.ShapeDtypeStruct((s, c), BF16),
        scratch_shapes=[pltpu.VMEM((tm + 2 * POOL_HALO, c), F32)],
        compiler_params=_params("parallel"),
        name="pool_mixer",
    )(u, u, u, w_pool, pool_scale.reshape(1, c))


def _attn_kernel(q_ref, *refs):
    k_refs = refs[:WIN_ROWS]
    v_refs = refs[WIN_ROWS:2 * WIN_ROWS]
    bias_ref = refs[2 * WIN_ROWS]
    o_ref = refs[2 * WIN_ROWS + 1]
    lane = lax.broadcasted_iota(jnp.int32, (GRID_W, LANES), 1)
    low = lane < HEAD_DIM
    scale = HEAD_DIM ** -0.5
    for pair in range(ATTN_WIDTH // LANES):
        cols = slice(pair * LANES, (pair + 1) * LANES)
        qp = q_ref[:, cols]
        k_all = jnp.concatenate([r[:, cols] for r in k_refs], axis=0)
        v_all = jnp.concatenate([r[:, cols] for r in v_refs], axis=0)
        outs = []
        for half in range(2):
            keep = low if half == 0 else jnp.logical_not(low)
            qm = jnp.where(keep, qp, jnp.zeros_like(qp))
            s = lax.dot_general(qm, k_all, (((1,), (1,)), ((), ())),
                                preferred_element_type=F32)
            s = s * scale + bias_ref[2 * pair + half]
            m = jnp.max(s, axis=-1, keepdims=True)
            e = jnp.exp(s - m)
            denom = jnp.sum(e, axis=-1, keepdims=True)
            o = jnp.dot(e.astype(BF16), v_all, preferred_element_type=F32)
            outs.append(o / denom)
        o_ref[:, cols] = jnp.where(low, outs[0], outs[1]).astype(o_ref.dtype)


def _attention_bias(rpb):
    col = jnp.arange(GRID_W)
    c0 = jnp.clip(col - WIN_COLS // 2, 0, GRID_W - WIN_COLS)
    in_win = (col[None, :] >= c0[:, None]) & (col[None, :] < c0[:, None] + WIN_COLS)
    dc = jnp.clip(col[None, :] - col[:, None] + (WIN_COLS - 1), 0, 2 * WIN_COLS - 2)
    variant = jnp.arange(WIN_ROWS)
    dr = jnp.arange(WIN_ROWS)[None, :] - variant[:, None] + (WIN_ROWS - 1)
    tab = rpb[:, dr][..., dc]
    tab = jnp.where(in_win[None, None, None], tab, MASK_BIAS)
    tab = jnp.transpose(tab, (1, 0, 3, 2, 4))
    return tab.reshape(WIN_ROWS, N_HEADS, GRID_W, WIN_ROWS * GRID_W).astype(F32)


def _attention(qk, v, rpb):
    s = v.shape[0]
    rows = s // GRID_W
    assert rows >= WIN_ROWS

    def first_row(r):
        return jnp.clip(r - WIN_ROWS // 2, 0, rows - WIN_ROWS)

    qk3 = qk.reshape(rows, GRID_W, 2 * ATTN_WIDTH)
    v3 = v.reshape(rows, GRID_W, ATTN_WIDTH)
    k_specs = [pl.BlockSpec((None, GRID_W, ATTN_WIDTH), lambda r, i=i: (first_row(r) + i, 0, 1))
               for i in range(WIN_ROWS)]
    v_specs = [pl.BlockSpec((None, GRID_W, ATTN_WIDTH), lambda r, i=i: (first_row(r) + i, 0, 0))
               for i in range(WIN_ROWS)]
    bias = _attention_bias(rpb)
    return pl.pallas_call(
        _attn_kernel,
        grid=(rows,),
        in_specs=[pl.BlockSpec((GRID_W, ATTN_WIDTH), lambda r: (r, 0))] + k_specs + v_specs + [
            pl.BlockSpec((None, N_HEADS, GRID_W, WIN_ROWS * GRID_W),
                         lambda r: (r - first_row(r), 0, 0, 0))],
        out_specs=pl.BlockSpec((GRID_W, ATTN_WIDTH), lambda r: (r, 0)),
        out_shape=jax.ShapeDtypeStruct((s, ATTN_WIDTH), BF16),
        compiler_params=_params("parallel"),
        name="nbr_attention",
    )(qk, *([qk3] * WIN_ROWS), *([v3] * WIN_ROWS), bias)


def _merge_kernel(h_ref, a_ref, b_ref, wga_ref, wgb_ref, wba_ref, wbb_ref, o_ref):
    h = h_ref[...]
    ga = _sigmoid(jnp.dot(h, wga_ref[...], preferred_element_type=F32))
    out = ga * jnp.dot(a_ref[...], wba_ref[...], preferred_element_type=F32)
    gb = _sigmoid(jnp.dot(h, wgb_ref[...], preferred_element_type=F32))
    out = out + gb * jnp.dot(b_ref[...], wbb_ref[...], preferred_element_type=F32)
    o_ref[...] = out.astype(o_ref.dtype)


def _merge(h, a, b, w_gate, w_branch_pool, w_branch_attn, *, tm=1024, tn=512):
    m, d = h.shape
    nb = d // tn
    return pl.pallas_call(
        _merge_kernel,
        grid=(m // tm, nb),
        in_specs=[
            pl.BlockSpec((tm, d), lambda i, j: (i, 0), pipeline_mode=pl.Buffered(1)),
            pl.BlockSpec((tm, POOL_WIDTH), lambda i, j: (i, 0), pipeline_mode=pl.Buffered(1)),
            pl.BlockSpec((tm, ATTN_WIDTH), lambda i, j: (i, 0), pipeline_mode=pl.Buffered(1)),
            pl.BlockSpec((d, tn), lambda i, j: (0, j)),
            pl.BlockSpec((d, tn), lambda i, j: (0, j + nb)),
            pl.BlockSpec((POOL_WIDTH, tn), lambda i, j: (0, j)),
            pl.BlockSpec((ATTN_WIDTH, tn), lambda i, j: (0, j)),
        ],
        out_specs=pl.BlockSpec((tm, tn), lambda i, j: (i, j)),
        out_shape=jax.ShapeDtypeStruct((m, d), BF16),
        compiler_params=_params("parallel", "parallel"),
        name="gated_merge",
    )(h, a, b, w_gate, w_gate, w_branch_pool, w_branch_attn)


def _ple_kernel(x_ref, p_ref, g_ref, wd_ref, wu_ref, wp_ref, o_ref):
    x = x_ref[...]
    ms = jnp.mean(x * x, axis=-1, keepdims=True)
    h = (x * lax.rsqrt(ms + EPS) * g_ref[...]).astype(BF16)
    t = jnp.dot(h, wd_ref[...], preferred_element_type=F32)
    gate = _sigmoid(jnp.dot(t.astype(BF16), wu_ref[...], preferred_element_type=F32))
    proj = jnp.dot(p_ref[...].astype(BF16), wp_ref[...], preferred_element_type=F32)
    o_ref[...] = x + gate * proj


def _ple(x, p, g, wd, wu, wp, *, tm=256):
    s, d = x.shape
    r = wd.shape[1]
    return pl.pallas_call(
        _ple_kernel,
        grid=(s // tm,),
        in_specs=[
            pl.BlockSpec((tm, d), lambda i: (i, 0)),
            pl.BlockSpec((tm, PLE_DIM), lambda i: (i, 0)),
            pl.BlockSpec((1, d), lambda i: (0, 0)),
            pl.BlockSpec((d, r), lambda i: (0, 0)),
            pl.BlockSpec((r, d), lambda i: (0, 0)),
            pl.BlockSpec((PLE_DIM, d), lambda i: (0, 0)),
        ],
        out_specs=pl.BlockSpec((tm, d), lambda i: (i, 0)),
        out_shape=jax.ShapeDtypeStruct((s, d), F32),
        compiler_params=_params("parallel"),
        name="ple",
    )(x, p, g.reshape(1, d), wd, wu, wp)


def kernel(x, p, norm_mix, w_in, w_pool, pool_scale, q_norm, k_norm, rpb,
           w_branch_pool, w_branch_attn, w_gate, w_out, norm_mlp, w_up, w_down,
           norm_ple, w_ple_gate_down, w_ple_gate_up, w_ple_proj):
    batch, seq, d = x.shape
    assert batch == 1
    depth = w_in.shape[0]
    xs = x.reshape(seq, d)
    heads_per_block = ATTN_WIDTH // HEAD_DIM
    for i in range(depth):
        w_in_b = w_in[i].astype(BF16)
        qk_gain = jnp.stack([jnp.tile(q_norm[i], heads_per_block),
                             jnp.tile(k_norm[i], heads_per_block)]).reshape(2, 1, ATTN_WIDTH)

        h = _rmsnorm(xs, norm_mix[i])
        u = _matmul(_mm_plain_kernel, h, w_in_b, n_cols=POOL_WIDTH, out_dtype=F32,
                    name="in_proj_pool")
        qk = _matmul(_mm_headnorm_kernel, h, w_in_b, n_cols=2 * ATTN_WIDTH, col_off=POOL_WIDTH,
                     out_dtype=BF16, gain=qk_gain, name="in_proj_qk")
        v = _matmul(_mm_plain_kernel, h, w_in_b, n_cols=ATTN_WIDTH,
                    col_off=POOL_WIDTH + 2 * ATTN_WIDTH, out_dtype=BF16, name="in_proj_v")
        a = _pool_mixer(u, w_pool[i].astype(BF16), pool_scale[i])
        b = _attention(qk, v, rpb[i])
        merged = _merge(h, a, b, w_gate[i].astype(BF16), w_branch_pool[i].astype(BF16),
                        w_branch_attn[i].astype(BF16))
        xs = _matmul(_mm_residual_kernel, merged, w_out[i].astype(BF16), n_cols=d,
                     out_dtype=F32, residual=xs, name="out_proj")

        h = _rmsnorm(xs, norm_mlp[i])
        up = _matmul(_mm_relu2_kernel, h, w_up[i].astype(BF16), n_cols=w_up.shape[2], out_dtype=BF16,
                     name="mlp_up")
        xs = _matmul_ksplit_residual(up, w_down[i].astype(BF16), xs, name="mlp_down")

        xs = _ple(xs, p[i].reshape(seq, PLE_DIM), norm_ple[i],
                  w_ple_gate_down[i].astype(BF16), w_ple_gate_up[i].astype(BF16),
                  w_ple_proj[i].astype(BF16))
    return xs.reshape(batch, seq, d)
```

```python
import functools
import math

import jax
import jax.numpy as jnp
from jax import lax
from jax.experimental import pallas as pl
from jax.experimental.pallas import tpu as pltpu

POOL_GROUPS = 4
POOL_WIDTH = 1024
POOL_GROUP_DIM = 256
POOL_WINDOWS = (2, 4, 8, 16)
POOL_HALO = 8
N_HEADS = 16
HEAD_DIM = 64
ATTN_WIDTH = 1024
GRID_W = 64
WIN_ROWS = 8
WIN_COLS = 16
PLE_DIM = 256
EPS = 1e-6

LANES = 128
VMEM_LIMIT = 60 * 1024 * 1024
MASK_BIAS = -1e30
LOG2E = math.log2(math.e)

ATTN_ROWS = 8
ATTN_SPAN = ATTN_ROWS + WIN_ROWS - 1
ROW_PAIRS = WIN_ROWS // 2
BIAS_PAIRS = 2 * WIN_ROWS - 2

F32 = jnp.float32
BF16 = jnp.bfloat16


def _params(*sem):
    return pltpu.CompilerParams(dimension_semantics=sem, vmem_limit_bytes=VMEM_LIMIT)


def _sigmoid(x):
    return 1.0 / (1.0 + jnp.exp(-x))


def _rmsnorm_rows(x, g):
    ms = jnp.mean(x * x, axis=-1, keepdims=True)
    return x * lax.rsqrt(ms + EPS) * g


def _rmsnorm_kernel(x_ref, g_ref, o_ref):
    o_ref[...] = _rmsnorm_rows(x_ref[...], g_ref[...]).astype(o_ref.dtype)


def _rmsnorm(x, g, *, tm=512):
    s, d = x.shape
    return pl.pallas_call(
        _rmsnorm_kernel,
        grid=(s // tm,),
        in_specs=[pl.BlockSpec((tm, d), lambda i: (i, 0)),
                  pl.BlockSpec((1, d), lambda i: (0, 0))],
        out_specs=pl.BlockSpec((tm, d), lambda i: (i, 0)),
        out_shape=jax.ShapeDtypeStruct((s, d), BF16),
        compiler_params=_params("parallel"),
        name="rmsnorm",
    )(x, g.reshape(1, d))


def _row_index(jj, i):
    return jnp.where(jj == 0, 0, i)


def _col_index(jj):
    return jnp.maximum(jj - 1, 0)


def _stage_chunk(wc_ref, wb_ref):
    jj, i = pl.program_id(0), pl.program_id(1)
    ck = wc_ref.shape[0]
    wb_ref[jj % 2, pl.ds(pl.multiple_of(i * ck, ck), ck), :] = wc_ref[...].astype(BF16)


def _staged_weight(wb_ref):
    return wb_ref[(pl.program_id(0) - 1) % 2]


def _head_rmsnorm(acc, g):
    lane = lax.broadcasted_iota(jnp.int32, (1, LANES), 1)
    low = lane < HEAD_DIM
    outs = []
    for c in range(acc.shape[1] // LANES):
        a = acc[:, c * LANES:(c + 1) * LANES]
        sq = a * a
        s_low = jnp.sum(jnp.where(low, sq, 0.0), axis=-1, keepdims=True)
        s_high = jnp.sum(jnp.where(low, 0.0, sq), axis=-1, keepdims=True)
        ms = jnp.where(low, s_low, s_high) * (1.0 / HEAD_DIM)
        outs.append(a * lax.rsqrt(ms + EPS) * g[:, c * LANES:(c + 1) * LANES])
    return jnp.concatenate(outs, axis=-1)


def _epi_plain(acc):
    return acc


def _epi_relu2(acc):
    y = jnp.maximum(acc, 0.0)
    return y * y


def _epi_residual(acc, x_ref):
    return x_ref[...] + acc


def _epi_headnorm(acc, g_ref):
    return _head_rmsnorm(acc, g_ref[...])


def _wmm_kernel(a_ref, wc_ref, *rest, epilogue):
    extra, o_ref, wb_ref = rest[:-2], rest[-2], rest[-1]
    jj = pl.program_id(0)

    @pl.when(jj == 0)
    def _():
        _stage_chunk(wc_ref, wb_ref)

    @pl.when(jj > 0)
    def _():
        _stage_chunk(wc_ref, wb_ref)
        acc = jnp.dot(a_ref[...], _staged_weight(wb_ref), preferred_element_type=F32)
        o_ref[...] = epilogue(acc, *extra).astype(o_ref.dtype)


def _weight_matmul(epilogue, a, w, layer, *, n_cols, col_off=0, k_block=0, kdim=None,
                   out_dtype, tm=1024, tn=1024, residual=None, gain=None, name):
    m = a.shape[0]
    kdim = a.shape[1] if kdim is None else kdim
    tm, tn = min(tm, m), min(tn, n_cols)
    ni, nj = m // tm, n_cols // tn
    ck = kdim // ni
    col0, row0 = col_off // tn, k_block * ni
    in_specs = [
        pl.BlockSpec((tm, kdim), lambda jj, i: (_row_index(jj, i), k_block)),
        pl.BlockSpec((None, ck, tn), lambda jj, i: (layer, row0 + i, col0 + jnp.minimum(jj, nj - 1))),
    ]
    args = [a, w]
    if residual is not None:
        in_specs.append(pl.BlockSpec((tm, tn), lambda jj, i: (_row_index(jj, i), _col_index(jj))))
        args.append(residual)
    if gain is not None:
        in_specs.append(pl.BlockSpec((None, 1, tn), lambda jj, i: (_col_index(jj), 0, 0)))
        args.append(gain)
    return pl.pallas_call(
        functools.partial(_wmm_kernel, epilogue=epilogue),
        grid=(nj + 1, ni),
        in_specs=in_specs,
        out_specs=pl.BlockSpec((tm, tn), lambda jj, i: (_row_index(jj, i), _col_index(jj))),
        out_shape=jax.ShapeDtypeStruct((m, n_cols), out_dtype),
        scratch_shapes=[pltpu.VMEM((2, kdim, tn), BF16)],
        compiler_params=_params("arbitrary", "arbitrary"),
        name=name,
    )(*args)


def _merge_kernel(h_ref, a_ref, b_ref, cga_ref, cgb_ref, cba_ref, cbb_ref, o_ref,
                  wga_ref, wgb_ref, wba_ref, wbb_ref):
    jj = pl.program_id(0)
    staged = ((cga_ref, wga_ref), (cgb_ref, wgb_ref), (cba_ref, wba_ref), (cbb_ref, wbb_ref))

    @pl.when(jj == 0)
    def _():
        for c_ref, w_ref in staged:
            _stage_chunk(c_ref, w_ref)

    @pl.when(jj > 0)
    def _():
        for c_ref, w_ref in staged:
            _stage_chunk(c_ref, w_ref)
        h = h_ref[...]
        ga = _sigmoid(jnp.dot(h, _staged_weight(wga_ref), preferred_element_type=F32))
        out = ga * jnp.dot(a_ref[...], _staged_weight(wba_ref), preferred_element_type=F32)
        gb = _sigmoid(jnp.dot(h, _staged_weight(wgb_ref), preferred_element_type=F32))
        out = out + gb * jnp.dot(b_ref[...], _staged_weight(wbb_ref), preferred_element_type=F32)
        o_ref[...] = out.astype(o_ref.dtype)


def _merge(h, a, b, w_gate, w_branch_pool, w_branch_attn, layer, *, tm=512, tn=512):
    m, d = h.shape
    tm, tn = min(tm, m), min(tn, d)
    ni, nj = m // tm, d // tn

    def rows(jj, i):
        return (_row_index(jj, i), 0)

    def chunk(col0):
        return lambda jj, i: (layer, i, col0 + jnp.minimum(jj, nj - 1))

    return pl.pallas_call(
        _merge_kernel,
        grid=(nj + 1, ni),
        in_specs=[
            pl.BlockSpec((tm, d), rows),
            pl.BlockSpec((tm, POOL_WIDTH), rows),
            pl.BlockSpec((tm, ATTN_WIDTH), rows),
            pl.BlockSpec((None, d // ni, tn), chunk(0)),
            pl.BlockSpec((None, d // ni, tn), chunk(nj)),
            pl.BlockSpec((None, POOL_WIDTH // ni, tn), chunk(0)),
            pl.BlockSpec((None, ATTN_WIDTH // ni, tn), chunk(0)),
        ],
        out_specs=pl.BlockSpec((tm, tn), lambda jj, i: (_row_index(jj, i), _col_index(jj))),
        out_shape=jax.ShapeDtypeStruct((m, d), BF16),
        scratch_shapes=[pltpu.VMEM((2, d, tn), BF16), pltpu.VMEM((2, d, tn), BF16),
                        pltpu.VMEM((2, POOL_WIDTH, tn), BF16), pltpu.VMEM((2, ATTN_WIDTH, tn), BF16)],
        compiler_params=_params("arbitrary", "arbitrary"),
        name="gated_merge",
    )(h, a, b, w_gate, w_gate, w_branch_pool, w_branch_attn)


def _pool_kernel(prev_ref, u_ref, next_ref, wp_ref, ps_ref, o_ref, buf_ref, *, seq):
    i = pl.program_id(0)
    tm = u_ref.shape[0]
    buf_ref[0:POOL_HALO, :] = jnp.where(i > 0, prev_ref[...], 0.0)
    buf_ref[POOL_HALO:POOL_HALO + tm, :] = u_ref[...]
    buf_ref[POOL_HALO + tm:, :] = jnp.where(i < pl.num_programs(0) - 1, next_ref[...], 0.0)
    t = i * tm + lax.broadcasted_iota(jnp.int32, (tm, 1), 0)
    for g, w in enumerate(POOL_WINDOWS):
        cols = slice(g * POOL_GROUP_DIM, (g + 1) * POOL_GROUP_DIM)
        total = None
        for o in range(-(w // 2), w // 2):
            piece = buf_ref[POOL_HALO + o:POOL_HALO + o + tm, cols]
            total = piece if total is None else total + piece
        cnt = (jnp.minimum(t + w // 2, seq) - jnp.maximum(t - w // 2, 0)).astype(F32)
        d = total / cnt - u_ref[:, cols]
        y = jnp.dot(d.astype(BF16), wp_ref[g], preferred_element_type=F32)
        o_ref[:, cols] = (y * ps_ref[:, cols]).astype(o_ref.dtype)


def _pool_mixer(u, w_pool, pool_scale, *, tm=512):
    s, c = u.shape
    halo_blocks = tm // POOL_HALO
    return pl.pallas_call(
        functools.partial(_pool_kernel, seq=s),
        grid=(s // tm,),
        in_specs=[
            pl.BlockSpec((POOL_HALO, c), lambda i: (jnp.maximum(i * halo_blocks - 1, 0), 0)),
            pl.BlockSpec((tm, c), lambda i: (i, 0)),
            pl.BlockSpec((POOL_HALO, c),
                         lambda i: (jnp.minimum((i + 1) * halo_blocks, s // POOL_HALO - 1), 0)),
            pl.BlockSpec((POOL_GROUPS, POOL_GROUP_DIM, POOL_GROUP_DIM), lambda i: (0, 0, 0)),
            pl.BlockSpec((1, c), lambda i: (0, 0)),
        ],
        out_specs=pl.BlockSpec((tm, c), lambda i: (i, 0)),
        out_shape=jax.ShapeDtypeStruct((s, c), BF16),
        scratch_shapes=[pltpu.VMEM((tm + 2 * POOL_HALO, c), F32)],
        compiler_params=_params("parallel"),
        name="pool_mixer",
    )(u, u, u, w_pool, pool_scale.reshape(1, c))


def _first_key_row(r, rows):
    return jnp.clip(r - WIN_ROWS // 2, 0, rows - WIN_ROWS)


def _span_start(blk, rows):
    return jnp.clip(blk * ATTN_ROWS - WIN_ROWS // 2, 0, rows - ATTN_SPAN)


def _attn_kernel(q_ref, k_ref, v_ref, bias_ref, o_ref, *, rows):
    blk = pl.program_id(0)
    span0 = _span_start(blk, rows)
    lane = lax.broadcasted_iota(jnp.int32, (GRID_W, LANES), 1)
    low = lane < HEAD_DIM
    n_keys = WIN_ROWS * GRID_W

    def one_query_row(a, carry):
        r = blk * ATTN_ROWS + a
        r0 = _first_key_row(r, rows)
        off = r0 - span0
        variant = r - r0
        q_rows = pl.ds(pl.multiple_of(a * GRID_W, GRID_W), GRID_W)

        def scores(h):
            pair, half = divmod(h, 2)
            cols = slice(pair * LANES, (pair + 1) * LANES)
            qp = q_ref[q_rows, cols]
            qm = jnp.where(low if half == 0 else jnp.logical_not(low), qp, jnp.zeros_like(qp))
            k_all = k_ref[pl.ds(off, WIN_ROWS), :, cols].reshape(n_keys, LANES)
            s = lax.dot_general(qm, k_all, (((1,), (1,)), ((), ())), preferred_element_type=F32)
            bias = jnp.concatenate(
                [bias_ref[h * BIAS_PAIRS + 2 * m + (WIN_ROWS - 1) - variant] for m in range(ROW_PAIRS)],
                axis=-1)
            return s * (HEAD_DIM ** -0.5 * LOG2E) + bias

        def weights(s):
            e = jnp.exp2(s - jnp.max(s, axis=-1, keepdims=True))
            return e.astype(BF16), jnp.sum(e, axis=-1, keepdims=True)

        def context(h, e, denom):
            cols = slice((h // 2) * LANES, (h // 2 + 1) * LANES)
            v_all = v_ref[pl.ds(off, WIN_ROWS), :, cols].reshape(n_keys, LANES)
            return jnp.dot(e, v_all, preferred_element_type=F32) / denom

        pending = {0: scores(0), 1: scores(1)}
        outs = {}
        for h in range(N_HEADS):
            e, denom = weights(pending.pop(h))
            if h + 2 < N_HEADS:
                pending[h + 2] = scores(h + 2)
            outs[h] = context(h, e, denom)
            if h % 2 == 1:
                cols = slice((h // 2) * LANES, (h // 2 + 1) * LANES)
                o_ref[q_rows, cols] = jnp.where(low, outs.pop(h - 1), outs.pop(h)).astype(o_ref.dtype)
        return carry

    lax.fori_loop(0, ATTN_ROWS, one_query_row, 0)


def _attention_bias(rpb):
    col = jnp.arange(GRID_W)
    c0 = jnp.clip(col - WIN_COLS // 2, 0, GRID_W - WIN_COLS)
    in_win = (col[None, :] >= c0[:, None]) & (col[None, :] < c0[:, None] + WIN_COLS)
    dc = jnp.clip(col[None, :] - col[:, None] + (WIN_COLS - 1), 0, 2 * WIN_COLS - 2)
    tab = jnp.where(in_win[None, None], rpb[:, :, dc] * LOG2E, MASK_BIAS)
    two = jnp.concatenate([tab[:, :-1], tab[:, 1:]], axis=-1)
    return two.reshape(N_HEADS * BIAS_PAIRS, GRID_W, 2 * GRID_W).astype(F32)


def _attention(qk, v, rpb):
    s = v.shape[0]
    rows = s // GRID_W
    assert rows >= ATTN_SPAN and rows % ATTN_ROWS == 0
    qk3 = qk.reshape(rows, GRID_W, 2 * ATTN_WIDTH)
    v3 = v.reshape(rows, GRID_W, ATTN_WIDTH)
    tq = ATTN_ROWS * GRID_W
    span_block = (pl.Element(ATTN_SPAN), pl.Element(GRID_W), pl.Element(ATTN_WIDTH))
    return pl.pallas_call(
        functools.partial(_attn_kernel, rows=rows),
        grid=(rows // ATTN_ROWS,),
        in_specs=[
            pl.BlockSpec((tq, ATTN_WIDTH), lambda b: (b, 0)),
            pl.BlockSpec(span_block, lambda b: (_span_start(b, rows), 0, ATTN_WIDTH)),
            pl.BlockSpec(span_block, lambda b: (_span_start(b, rows), 0, 0)),
            pl.BlockSpec((N_HEADS * BIAS_PAIRS, GRID_W, 2 * GRID_W), lambda b: (0, 0, 0),
                         pipeline_mode=pl.Buffered(1)),
        ],
        out_specs=pl.BlockSpec((tq, ATTN_WIDTH), lambda b: (b, 0)),
        out_shape=jax.ShapeDtypeStruct((s, ATTN_WIDTH), BF16),
        compiler_params=_params("parallel"),
        name="nbr_attention",
    )(qk, qk3, v3, _attention_bias(rpb))


def _ple_kernel(x_ref, p_ref, g_ref, wd_ref, wu_ref, wp_ref, *rest):
    x = x_ref[...]
    h = _rmsnorm_rows(x, g_ref[...]).astype(BF16)
    t = jnp.dot(h, wd_ref[...], preferred_element_type=F32)
    gate = _sigmoid(jnp.dot(t.astype(BF16), wu_ref[...], preferred_element_type=F32))
    proj = jnp.dot(p_ref[...].astype(BF16), wp_ref[...], preferred_element_type=F32)
    y = x + gate * proj
    if len(rest) == 3:
        gn_ref, o_ref, hn_ref = rest
        hn_ref[...] = _rmsnorm_rows(y, gn_ref[...]).astype(hn_ref.dtype)
    else:
        (o_ref,) = rest
    o_ref[...] = y


def _ple(x, p, g, wd, wu, wp, g_next=None, *, tm=256):
    s, d = x.shape
    r = wd.shape[1]
    row = lambda i: (i, 0)
    fixed = lambda i: (0, 0)
    in_specs = [
        pl.BlockSpec((tm, d), row),
        pl.BlockSpec((tm, PLE_DIM), row),
        pl.BlockSpec((1, d), fixed),
        pl.BlockSpec((d, r), fixed),
        pl.BlockSpec((r, d), fixed),
        pl.BlockSpec((PLE_DIM, d), fixed),
    ]
    args = [x, p, g.reshape(1, d), wd, wu, wp]
    out_specs = pl.BlockSpec((tm, d), row)
    out_shape = jax.ShapeDtypeStruct((s, d), F32)
    if g_next is not None:
        in_specs.append(pl.BlockSpec((1, d), fixed))
        args.append(g_next.reshape(1, d))
        out_specs = (out_specs, pl.BlockSpec((tm, d), row))
        out_shape = (out_shape, jax.ShapeDtypeStruct((s, d), BF16))
    return pl.pallas_call(
        _ple_kernel,
        grid=(s // tm,),
        in_specs=in_specs,
        out_specs=out_specs,
        out_shape=out_shape,
        compiler_params=_params("parallel"),
        name="ple",
    )(*args)


def kernel(x, p, norm_mix, w_in, w_pool, pool_scale, q_norm, k_norm, rpb,
           w_branch_pool, w_branch_attn, w_gate, w_out, norm_mlp, w_up, w_down,
           norm_ple, w_ple_gate_down, w_ple_gate_up, w_ple_proj):
    batch, seq, d = x.shape
    assert batch == 1
    depth = w_in.shape[0]
    d_ff = w_up.shape[2]
    xs = x.reshape(seq, d)
    heads_per_block = ATTN_WIDTH // HEAD_DIM
    h = _rmsnorm(xs, norm_mix[0])
    for i in range(depth):
        qk_gain = jnp.stack([jnp.tile(q_norm[i], heads_per_block),
                             jnp.tile(k_norm[i], heads_per_block)]).reshape(2, 1, ATTN_WIDTH)
        u = _weight_matmul(_epi_plain, h, w_in, i, n_cols=POOL_WIDTH, out_dtype=F32,
                           name="in_proj_pool")
        qk = _weight_matmul(_epi_headnorm, h, w_in, i, n_cols=2 * ATTN_WIDTH, col_off=POOL_WIDTH,
                            out_dtype=BF16, gain=qk_gain, name="in_proj_qk")
        v = _weight_matmul(_epi_plain, h, w_in, i, n_cols=ATTN_WIDTH,
                           col_off=POOL_WIDTH + 2 * ATTN_WIDTH, out_dtype=BF16, name="in_proj_v")
        a = _pool_mixer(u, w_pool[i].astype(BF16), pool_scale[i])
        b = _attention(qk, v, rpb[i])
        merged = _merge(h, a, b, w_gate, w_branch_pool, w_branch_attn, i)
        xs = _weight_matmul(_epi_residual, merged, w_out, i, n_cols=d, out_dtype=F32,
                            residual=xs, name="out_proj")

        h = _rmsnorm(xs, norm_mlp[i])
        up = _weight_matmul(_epi_relu2, h, w_up, i, n_cols=d_ff, out_dtype=BF16, name="mlp_up")
        for kb in range(d_ff // d):
            xs = _weight_matmul(_epi_residual, up, w_down, i, n_cols=d, k_block=kb, kdim=d,
                                out_dtype=F32, residual=xs, name="mlp_down")

        ple_w = (w_ple_gate_down[i].astype(BF16), w_ple_gate_up[i].astype(BF16),
                 w_ple_proj[i].astype(BF16))
        p_i = p[i].reshape(seq, PLE_DIM)
        if i + 1 < depth:
            xs, h = _ple(xs, p_i, norm_ple[i], *ple_w, g_next=norm_mix[i + 1])
        else:
            xs = _ple(xs, p_i, norm_ple[i], *ple_w)
    return xs.reshape(batch, seq, d)
```

```python
import functools
import math

import jax
import jax.numpy as jnp
from jax import lax
from jax.experimental import pallas as pl
from jax.experimental.pallas import tpu as pltpu

POOL_GROUPS = 4
POOL_WIDTH = 1024
POOL_GROUP_DIM = 256
POOL_WINDOWS = (2, 4, 8, 16)
POOL_HALO = 8
N_HEADS = 16
HEAD_DIM = 64
ATTN_WIDTH = 1024
GRID_W = 64
WIN_ROWS = 8
WIN_COLS = 16
PLE_DIM = 256
EPS = 1e-6

LANES = 128
VMEM_LIMIT = 60 * 1024 * 1024
MASK_BIAS = -1e30
LOG2E = math.log2(math.e)

ATTN_ROWS = 8
ATTN_SPAN = ATTN_ROWS + WIN_ROWS - 1
ROWS_PER_ITER = 2
HEAD_SKEW = 8
ROW_PAIRS = WIN_ROWS // 2
BIAS_PAIRS = 2 * WIN_ROWS - 2

F32 = jnp.float32
BF16 = jnp.bfloat16


def _params(*sem):
    return pltpu.CompilerParams(dimension_semantics=sem, vmem_limit_bytes=VMEM_LIMIT)


def _sigmoid(x):
    return 1.0 / (1.0 + jnp.exp(-x))


def _rmsnorm_rows(x, g):
    ms = jnp.mean(x * x, axis=-1, keepdims=True)
    return x * lax.rsqrt(ms + EPS) * g


def _rmsnorm_kernel(x_ref, g_ref, o_ref):
    o_ref[...] = _rmsnorm_rows(x_ref[...], g_ref[...]).astype(o_ref.dtype)


def _rmsnorm(x, g, *, tm=512):
    s, d = x.shape
    return pl.pallas_call(
        _rmsnorm_kernel,
        grid=(s // tm,),
        in_specs=[pl.BlockSpec((tm, d), lambda i: (i, 0)),
                  pl.BlockSpec((1, d), lambda i: (0, 0))],
        out_specs=pl.BlockSpec((tm, d), lambda i: (i, 0)),
        out_shape=jax.ShapeDtypeStruct((s, d), BF16),
        compiler_params=_params("parallel"),
        name="rmsnorm",
    )(x, g.reshape(1, d))


def _row_index(jj, i):
    return jnp.where(jj == 0, 0, i)


def _col_index(jj):
    return jnp.maximum(jj - 1, 0)


def _stage_chunk(wc_ref, wb_ref):
    jj, i = pl.program_id(0), pl.program_id(1)
    ck = wc_ref.shape[0]
    wb_ref[jj % 2, pl.ds(pl.multiple_of(i * ck, ck), ck), :] = wc_ref[...].astype(BF16)


def _staged_weight(wb_ref):
    return wb_ref[(pl.program_id(0) - 1) % 2]


def _head_rmsnorm(acc, g):
    lane = lax.broadcasted_iota(jnp.int32, (1, LANES), 1)
    low = lane < HEAD_DIM
    outs = []
    for c in range(acc.shape[1] // LANES):
        a = acc[:, c * LANES:(c + 1) * LANES]
        sq = a * a
        s_low = jnp.sum(jnp.where(low, sq, 0.0), axis=-1, keepdims=True)
        s_high = jnp.sum(jnp.where(low, 0.0, sq), axis=-1, keepdims=True)
        ms = jnp.where(low, s_low, s_high) * (1.0 / HEAD_DIM)
        outs.append(a * lax.rsqrt(ms + EPS) * g[:, c * LANES:(c + 1) * LANES])
    return jnp.concatenate(outs, axis=-1)


def _epi_plain(acc):
    return acc


def _epi_relu2(acc):
    y = jnp.maximum(acc, 0.0)
    return y * y


def _epi_residual(acc, x_ref):
    return x_ref[...] + acc


def _epi_headnorm(acc, g_ref):
    return _head_rmsnorm(acc, g_ref[...])


def _wmm_kernel(a_ref, wc_ref, *rest, epilogue):
    extra, o_ref, wb_ref = rest[:-2], rest[-2], rest[-1]
    jj = pl.program_id(0)

    @pl.when(jj == 0)
    def _():
        _stage_chunk(wc_ref, wb_ref)

    @pl.when(jj > 0)
    def _():
        acc = jnp.dot(a_ref[...], _staged_weight(wb_ref), preferred_element_type=F32)
        o_ref[...] = epilogue(acc, *extra).astype(o_ref.dtype)
        _stage_chunk(wc_ref, wb_ref)


def _weight_matmul(epilogue, a, w, layer, *, n_cols, col_off=0, k_block=0, kdim=None,
                   out_dtype, tm=1024, tn=1024, residual=None, gain=None, name):
    m = a.shape[0]
    kdim = a.shape[1] if kdim is None else kdim
    tm, tn = min(tm, m), min(tn, n_cols)
    ni, nj = m // tm, n_cols // tn
    ck = kdim // ni
    col0, row0 = col_off // tn, k_block * ni
    in_specs = [
        pl.BlockSpec((tm, kdim), lambda jj, i: (_row_index(jj, i), k_block)),
        pl.BlockSpec((None, ck, tn), lambda jj, i: (layer, row0 + i, col0 + jnp.minimum(jj, nj - 1))),
    ]
    args = [a, w]
    if residual is not None:
        in_specs.append(pl.BlockSpec((tm, tn), lambda jj, i: (_row_index(jj, i), _col_index(jj))))
        args.append(residual)
    if gain is not None:
        in_specs.append(pl.BlockSpec((None, 1, tn), lambda jj, i: (_col_index(jj), 0, 0)))
        args.append(gain)
    return pl.pallas_call(
        functools.partial(_wmm_kernel, epilogue=epilogue),
        grid=(nj + 1, ni),
        in_specs=in_specs,
        out_specs=pl.BlockSpec((tm, tn), lambda jj, i: (_row_index(jj, i), _col_index(jj))),
        out_shape=jax.ShapeDtypeStruct((m, n_cols), out_dtype),
        scratch_shapes=[pltpu.VMEM((2, kdim, tn), BF16)],
        compiler_params=_params("arbitrary", "arbitrary"),
        name=name,
    )(*args)


def _merge_kernel(h_ref, a_ref, b_ref, cga_ref, cgb_ref, cba_ref, cbb_ref, o_ref,
                  wga_ref, wgb_ref, wba_ref, wbb_ref):
    jj = pl.program_id(0)
    staged = ((cga_ref, wga_ref), (cgb_ref, wgb_ref), (cba_ref, wba_ref), (cbb_ref, wbb_ref))

    @pl.when(jj == 0)
    def _():
        for c_ref, w_ref in staged:
            _stage_chunk(c_ref, w_ref)

    @pl.when(jj > 0)
    def _():
        h = h_ref[...]
        ga = _sigmoid(jnp.dot(h, _staged_weight(wga_ref), preferred_element_type=F32))
        out = ga * jnp.dot(a_ref[...], _staged_weight(wba_ref), preferred_element_type=F32)
        gb = _sigmoid(jnp.dot(h, _staged_weight(wgb_ref), preferred_element_type=F32))
        out = out + gb * jnp.dot(b_ref[...], _staged_weight(wbb_ref), preferred_element_type=F32)
        o_ref[...] = out.astype(o_ref.dtype)
        for c_ref, w_ref in staged:
            _stage_chunk(c_ref, w_ref)


def _merge(h, a, b, w_gate, w_branch_pool, w_branch_attn, layer, *, tm=1024, tn=512):
    m, d = h.shape
    tm, tn = min(tm, m), min(tn, d)
    ni, nj = m // tm, d // tn

    def rows(jj, i):
        return (_row_index(jj, i), 0)

    def chunk(col0):
        return lambda jj, i: (layer, i, col0 + jnp.minimum(jj, nj - 1))

    return pl.pallas_call(
        _merge_kernel,
        grid=(nj + 1, ni),
        in_specs=[
            pl.BlockSpec((tm, d), rows),
            pl.BlockSpec((tm, POOL_WIDTH), rows),
            pl.BlockSpec((tm, ATTN_WIDTH), rows),
            pl.BlockSpec((None, d // ni, tn), chunk(0)),
            pl.BlockSpec((None, d // ni, tn), chunk(nj)),
            pl.BlockSpec((None, POOL_WIDTH // ni, tn), chunk(0)),
            pl.BlockSpec((None, ATTN_WIDTH // ni, tn), chunk(0)),
        ],
        out_specs=pl.BlockSpec((tm, tn), lambda jj, i: (_row_index(jj, i), _col_index(jj))),
        out_shape=jax.ShapeDtypeStruct((m, d), BF16),
        scratch_shapes=[pltpu.VMEM((2, d, tn), BF16), pltpu.VMEM((2, d, tn), BF16),
                        pltpu.VMEM((2, POOL_WIDTH, tn), BF16), pltpu.VMEM((2, ATTN_WIDTH, tn), BF16)],
        compiler_params=_params("arbitrary", "arbitrary"),
        name="gated_merge",
    )(h, a, b, w_gate, w_gate, w_branch_pool, w_branch_attn)


def _pool_kernel(prev_ref, u_ref, next_ref, wp_ref, ps_ref, o_ref, buf_ref, *, seq):
    i = pl.program_id(0)
    tm = u_ref.shape[0]
    buf_ref[0:POOL_HALO, :] = jnp.where(i > 0, prev_ref[...], 0.0)
    buf_ref[POOL_HALO:POOL_HALO + tm, :] = u_ref[...]
    buf_ref[POOL_HALO + tm:, :] = jnp.where(i < pl.num_programs(0) - 1, next_ref[...], 0.0)
    t = i * tm + lax.broadcasted_iota(jnp.int32, (tm, 1), 0)
    for g, w in enumerate(POOL_WINDOWS):
        cols = slice(g * POOL_GROUP_DIM, (g + 1) * POOL_GROUP_DIM)
        total = None
        for o in range(-(w // 2), w // 2):
            piece = buf_ref[POOL_HALO + o:POOL_HALO + o + tm, cols]
            total = piece if total is None else total + piece
        cnt = (jnp.minimum(t + w // 2, seq) - jnp.maximum(t - w // 2, 0)).astype(F32)
        d = total / cnt - u_ref[:, cols]
        y = jnp.dot(d.astype(BF16), wp_ref[g], preferred_element_type=F32)
        o_ref[:, cols] = (y * ps_ref[:, cols]).astype(o_ref.dtype)


def _pool_mixer(u, w_pool, pool_scale, *, tm=512):
    s, c = u.shape
    halo_blocks = tm // POOL_HALO
    return pl.pallas_call(
        functools.partial(_pool_kernel, seq=s),
        grid=(s // tm,),
        in_specs=[
            pl.BlockSpec((POOL_HALO, c), lambda i: (jnp.maximum(i * halo_blocks - 1, 0), 0)),
            pl.BlockSpec((tm, c), lambda i: (i, 0)),
            pl.BlockSpec((POOL_HALO, c),
                         lambda i: (jnp.minimum((i + 1) * halo_blocks, s // POOL_HALO - 1), 0)),
            pl.BlockSpec((POOL_GROUPS, POOL_GROUP_DIM, POOL_GROUP_DIM), lambda i: (0, 0, 0)),
            pl.BlockSpec((1, c), lambda i: (0, 0)),
        ],
        out_specs=pl.BlockSpec((tm, c), lambda i: (i, 0)),
        out_shape=jax.ShapeDtypeStruct((s, c), BF16),
        scratch_shapes=[pltpu.VMEM((tm + 2 * POOL_HALO, c), F32)],
        compiler_params=_params("parallel"),
        name="pool_mixer",
    )(u, u, u, w_pool, pool_scale.reshape(1, c))


def _first_key_row(r, rows):
    return jnp.clip(r - WIN_ROWS // 2, 0, rows - WIN_ROWS)


def _span_start(blk, rows):
    return jnp.clip(blk * ATTN_ROWS - WIN_ROWS // 2, 0, rows - ATTN_SPAN)


def _attn_kernel(q_ref, k_ref, v_ref, bias_ref, o_ref, *, rows):
    blk = pl.program_id(0)
    span0 = _span_start(blk, rows)
    lane = lax.broadcasted_iota(jnp.int32, (GRID_W, LANES), 1)
    low = lane < HEAD_DIM
    n_keys = WIN_ROWS * GRID_W

    def query_rows(it, carry):
        geometry = []
        for sub in range(ROWS_PER_ITER):
            a = it * ROWS_PER_ITER + sub
            r = blk * ATTN_ROWS + a
            r0 = _first_key_row(r, rows)
            geometry.append((pl.ds(pl.multiple_of(a * GRID_W, GRID_W), GRID_W), r0 - span0, r - r0))

        def scores(sub, h):
            q_rows, off, variant = geometry[sub]
            pair, half = divmod(h, 2)
            cols = slice(pair * LANES, (pair + 1) * LANES)
            qp = q_ref[q_rows, cols]
            qm = jnp.where(low if half == 0 else jnp.logical_not(low), qp, jnp.zeros_like(qp))
            k_all = k_ref[pl.ds(off, WIN_ROWS), :, cols].reshape(n_keys, LANES)
            s = lax.dot_general(qm, k_all, (((1,), (1,)), ((), ())), preferred_element_type=F32)
            bias = jnp.concatenate(
                [bias_ref[h * BIAS_PAIRS + 2 * m + (WIN_ROWS - 1) - variant] for m in range(ROW_PAIRS)],
                axis=-1)
            return s * (HEAD_DIM ** -0.5 * LOG2E) + bias

        def weights(s):
            e = jnp.exp2(s - jnp.max(s, axis=-1, keepdims=True))
            return e.astype(BF16), jnp.sum(e, axis=-1, keepdims=True)

        def context(sub, h, e, denom):
            _, off, _ = geometry[sub]
            cols = slice((h // 2) * LANES, (h // 2 + 1) * LANES)
            v_all = v_ref[pl.ds(off, WIN_ROWS), :, cols].reshape(n_keys, LANES)
            return jnp.dot(e, v_all, preferred_element_type=F32) / denom

        chains = [(sub, h) for sub in range(ROWS_PER_ITER) for h in range(N_HEADS)]
        pending = {c: scores(*c) for c in chains[:HEAD_SKEW]}
        outs = {}
        for n, (sub, h) in enumerate(chains):
            e, denom = weights(pending.pop((sub, h)))
            if n + HEAD_SKEW < len(chains):
                ahead = chains[n + HEAD_SKEW]
                pending[ahead] = scores(*ahead)
            outs[h] = context(sub, h, e, denom)
            if h % 2 == 1:
                cols = slice((h // 2) * LANES, (h // 2 + 1) * LANES)
                o_ref[geometry[sub][0], cols] = jnp.where(
                    low, outs.pop(h - 1), outs.pop(h)).astype(o_ref.dtype)
        return carry

    lax.fori_loop(0, ATTN_ROWS // ROWS_PER_ITER, query_rows, 0)


def _attention_bias(rpb):
    depth = rpb.shape[0]
    n_dc = 2 * WIN_COLS - 1
    col = jnp.arange(GRID_W)
    c0 = jnp.clip(col - WIN_COLS // 2, 0, GRID_W - WIN_COLS)
    in_win = (col[None, :] >= c0[:, None]) & (col[None, :] < c0[:, None] + WIN_COLS)
    dc = col[None, :] - col[:, None] + (WIN_COLS - 1)
    onehot = (jnp.arange(n_dc)[:, None, None] == dc[None]).astype(F32).reshape(n_dc, GRID_W * GRID_W)
    tab = jnp.dot(rpb.reshape(-1, n_dc), onehot, precision=lax.Precision.HIGHEST)
    tab = tab.reshape(depth, N_HEADS, 2 * WIN_ROWS - 1, GRID_W, GRID_W)
    tab = jnp.where(in_win, tab * LOG2E, MASK_BIAS)
    two = jnp.concatenate([tab[:, :, :-1], tab[:, :, 1:]], axis=-1)
    return two.reshape(depth, N_HEADS * BIAS_PAIRS, GRID_W, 2 * GRID_W)


def _attention(qk, v, bias, layer):
    s = v.shape[0]
    rows = s // GRID_W
    assert rows >= ATTN_SPAN and rows % ATTN_ROWS == 0
    qk3 = qk.reshape(rows, GRID_W, 2 * ATTN_WIDTH)
    v3 = v.reshape(rows, GRID_W, ATTN_WIDTH)
    tq = ATTN_ROWS * GRID_W
    span_block = (pl.Element(ATTN_SPAN), pl.Element(GRID_W), pl.Element(ATTN_WIDTH))
    return pl.pallas_call(
        functools.partial(_attn_kernel, rows=rows),
        grid=(rows // ATTN_ROWS,),
        in_specs=[
            pl.BlockSpec((tq, ATTN_WIDTH), lambda b: (b, 0)),
            pl.BlockSpec(span_block, lambda b: (_span_start(b, rows), 0, ATTN_WIDTH)),
            pl.BlockSpec(span_block, lambda b: (_span_start(b, rows), 0, 0)),
            pl.BlockSpec((None, N_HEADS * BIAS_PAIRS, GRID_W, 2 * GRID_W),
                         lambda b: (layer, 0, 0, 0), pipeline_mode=pl.Buffered(1)),
        ],
        out_specs=pl.BlockSpec((tq, ATTN_WIDTH), lambda b: (b, 0)),
        out_shape=jax.ShapeDtypeStruct((s, ATTN_WIDTH), BF16),
        compiler_params=_params("parallel"),
        name="nbr_attention",
    )(qk, qk3, v3, bias)


def _ple_kernel(x_ref, p_ref, g_ref, wd_ref, wu_ref, wp_ref, *rest):
    x = x_ref[...]
    h = _rmsnorm_rows(x, g_ref[...]).astype(BF16)
    t = jnp.dot(h, wd_ref[...], preferred_element_type=F32)
    gate = _sigmoid(jnp.dot(t.astype(BF16), wu_ref[...], preferred_element_type=F32))
    proj = jnp.dot(p_ref[...].astype(BF16), wp_ref[...], preferred_element_type=F32)
    y = x + gate * proj
    if len(rest) == 3:
        gn_ref, o_ref, hn_ref = rest
        hn_ref[...] = _rmsnorm_rows(y, gn_ref[...]).astype(hn_ref.dtype)
    else:
        (o_ref,) = rest
    o_ref[...] = y


def _ple(x, p, g, wd, wu, wp, g_next=None, *, tm=256):
    s, d = x.shape
    r = wd.shape[1]
    row = lambda i: (i, 0)
    fixed = lambda i: (0, 0)
    in_specs = [
        pl.BlockSpec((tm, d), row),
        pl.BlockSpec((tm, PLE_DIM), row),
        pl.BlockSpec((1, d), fixed),
        pl.BlockSpec((d, r), fixed),
        pl.BlockSpec((r, d), fixed),
        pl.BlockSpec((PLE_DIM, d), fixed),
    ]
    args = [x, p, g.reshape(1, d), wd, wu, wp]
    out_specs = pl.BlockSpec((tm, d), row)
    out_shape = jax.ShapeDtypeStruct((s, d), F32)
    if g_next is not None:
        in_specs.append(pl.BlockSpec((1, d), fixed))
        args.append(g_next.reshape(1, d))
        out_specs = (out_specs, pl.BlockSpec((tm, d), row))
        out_shape = (out_shape, jax.ShapeDtypeStruct((s, d), BF16))
    return pl.pallas_call(
        _ple_kernel,
        grid=(s // tm,),
        in_specs=in_specs,
        out_specs=out_specs,
        out_shape=out_shape,
        compiler_params=_params("parallel"),
        name="ple",
    )(*args)


def kernel(x, p, norm_mix, w_in, w_pool, pool_scale, q_norm, k_norm, rpb,
           w_branch_pool, w_branch_attn, w_gate, w_out, norm_mlp, w_up, w_down,
           norm_ple, w_ple_gate_down, w_ple_gate_up, w_ple_proj):
    batch, seq, d = x.shape
    assert batch == 1
    depth = w_in.shape[0]
    d_ff = w_up.shape[2]
    xs = x.reshape(seq, d)
    heads_per_block = ATTN_WIDTH // HEAD_DIM
    attn_bias = _attention_bias(rpb)
    h = _rmsnorm(xs, norm_mix[0])
    for i in range(depth):
        qk_gain = jnp.stack([jnp.tile(q_norm[i], heads_per_block),
                             jnp.tile(k_norm[i], heads_per_block)]).reshape(2, 1, ATTN_WIDTH)
        u = _weight_matmul(_epi_plain, h, w_in, i, n_cols=POOL_WIDTH, out_dtype=F32,
                           name="in_proj_pool")
        qk = _weight_matmul(_epi_headnorm, h, w_in, i, n_cols=2 * ATTN_WIDTH, col_off=POOL_WIDTH,
                            out_dtype=BF16, gain=qk_gain, name="in_proj_qk")
        v = _weight_matmul(_epi_plain, h, w_in, i, n_cols=ATTN_WIDTH,
                           col_off=POOL_WIDTH + 2 * ATTN_WIDTH, out_dtype=BF16, name="in_proj_v")
        a = _pool_mixer(u, w_pool[i].astype(BF16), pool_scale[i])
        b = _attention(qk, v, attn_bias, i)
        merged = _merge(h, a, b, w_gate, w_branch_pool, w_branch_attn, i)
        xs = _weight_matmul(_epi_residual, merged, w_out, i, n_cols=d, out_dtype=F32,
                            residual=xs, name="out_proj")

        h = _rmsnorm(xs, norm_mlp[i])
        up = _weight_matmul(_epi_relu2, h, w_up, i, n_cols=d_ff, out_dtype=BF16, name="mlp_up")
        for kb in range(d_ff // d):
            xs = _weight_matmul(_epi_residual, up, w_down, i, n_cols=d, k_block=kb, kdim=d,
                                out_dtype=F32, residual=xs, name="mlp_down")

        ple_w = (w_ple_gate_down[i].astype(BF16), w_ple_gate_up[i].astype(BF16),
                 w_ple_proj[i].astype(BF16))
        p_i = p[i].reshape(seq, PLE_DIM)
        if i + 1 < depth:
            xs, h = _ple(xs, p_i, norm_ple[i], *ple_w, g_next=norm_mix[i + 1])
        else:
            xs = _ple(xs, p_i, norm_ple[i], *ple_w)
    return xs.reshape(batch, seq, d)
```

```python
import functools
import math

import jax
import jax.numpy as jnp
from jax import lax
from jax.experimental import pallas as pl
from jax.experimental.pallas import tpu as pltpu

POOL_GROUPS = 4
POOL_WIDTH = 1024
POOL_GROUP_DIM = 256
POOL_WINDOWS = (2, 4, 8, 16)
POOL_HALO = 8
N_HEADS = 16
HEAD_DIM = 64
ATTN_WIDTH = 1024
GRID_W = 64
WIN_ROWS = 8
WIN_COLS = 16
PLE_DIM = 256
PLE_SUB_ROWS = 128
EPS = 1e-6

LANES = 128
VMEM_LIMIT = 60 * 1024 * 1024
MASK_BIAS = -1e30
LOG2E = math.log2(math.e)

ATTN_ROWS = 8
ATTN_SPAN = ATTN_ROWS + WIN_ROWS - 1
ROWS_PER_ITER = 2
HEAD_SKEW = 8
ROW_PAIRS = WIN_ROWS // 2
BIAS_PAIRS = 2 * WIN_ROWS - 2

F32 = jnp.float32
BF16 = jnp.bfloat16


def _params(*sem):
    return pltpu.CompilerParams(dimension_semantics=sem, vmem_limit_bytes=VMEM_LIMIT)


def _sigmoid(x):
    return 1.0 / (1.0 + jnp.exp2(x * -LOG2E))


def _rmsnorm_rows(x, g):
    ms = jnp.mean(x * x, axis=-1, keepdims=True)
    return x * lax.rsqrt(ms + EPS) * g


def _rmsnorm_kernel(x_ref, g_ref, o_ref):
    o_ref[...] = _rmsnorm_rows(x_ref[...], g_ref[...]).astype(o_ref.dtype)


def _rmsnorm(x, g, *, tm=512):
    s, d = x.shape
    return pl.pallas_call(
        _rmsnorm_kernel,
        grid=(s // tm,),
        in_specs=[pl.BlockSpec((tm, d), lambda i: (i, 0)),
                  pl.BlockSpec((1, d), lambda i: (0, 0))],
        out_specs=pl.BlockSpec((tm, d), lambda i: (i, 0)),
        out_shape=jax.ShapeDtypeStruct((s, d), BF16),
        compiler_params=_params("parallel"),
        name="rmsnorm",
    )(x, g.reshape(1, d))


def _row_index(jj, i):
    return jnp.where(jj == 0, 0, i)


def _col_index(jj):
    return jnp.maximum(jj - 1, 0)


def _stage_chunk(wc_ref, wb_ref):
    jj, i = pl.program_id(0), pl.program_id(1)
    ck = wc_ref.shape[0]
    wb_ref[jj % 2, pl.ds(pl.multiple_of(i * ck, ck), ck), :] = wc_ref[...].astype(BF16)


def _staged_weight(wb_ref):
    return wb_ref[(pl.program_id(0) - 1) % 2]


def _head_rmsnorm(acc, g):
    lane = lax.broadcasted_iota(jnp.int32, (1, LANES), 1)
    low = lane < HEAD_DIM
    outs = []
    for c in range(acc.shape[1] // LANES):
        a = acc[:, c * LANES:(c + 1) * LANES]
        sq = a * a
        s_low = jnp.sum(jnp.where(low, sq, 0.0), axis=-1, keepdims=True)
        s_high = jnp.sum(jnp.where(low, 0.0, sq), axis=-1, keepdims=True)
        ms = jnp.where(low, s_low, s_high) * (1.0 / HEAD_DIM)
        outs.append(a * lax.rsqrt(ms + EPS) * g[:, c * LANES:(c + 1) * LANES])
    return jnp.concatenate(outs, axis=-1)


def _epi_store(acc, ins, outs):
    outs[0][...] = acc.astype(outs[0].dtype)


def _epi_headnorm(acc, ins, outs):
    outs[0][...] = _head_rmsnorm(acc, ins[0][...]).astype(outs[0].dtype)


def _epi_residual(acc, ins, outs):
    outs[0][...] = ins[0][...] + acc


def _epi_residual_prenorm(acc, ins, outs):
    x_ref, g_ref = ins
    o_ref, hq_ref, ss_ref = outs
    y = x_ref[...] + acc
    o_ref[...] = y
    hq_ref[...] = (y * g_ref[...]).astype(hq_ref.dtype)
    ss_ref[...] = jnp.broadcast_to(jnp.sum(y * y, axis=-1, keepdims=True), ss_ref.shape)


def _epi_rowscale_relu2(acc, ins, outs, *, width):
    ss = ins[0][...]
    total = ss[:, 0:1]
    for part in range(1, ss.shape[1] // LANES):
        total = total + ss[:, part * LANES:part * LANES + 1]
    y = jnp.maximum(acc * lax.rsqrt(total / width + EPS), 0.0)
    outs[0][...] = (y * y).astype(outs[0].dtype)


def _wmm_kernel(a_ref, wc_ref, *rest, epilogue, n_in):
    ins, outs, wb_ref = rest[:n_in], rest[n_in:-1], rest[-1]
    jj = pl.program_id(0)

    @pl.when(jj == 0)
    def _():
        _stage_chunk(wc_ref, wb_ref)

    @pl.when(jj > 0)
    def _():
        acc = jnp.dot(a_ref[...], _staged_weight(wb_ref), preferred_element_type=F32)
        epilogue(acc, ins, outs)
        _stage_chunk(wc_ref, wb_ref)


def _weight_matmul(epilogue, a, w, layer, *, n_cols, out_dtypes, col_off=0, k_block=0, kdim=None,
                   tm=1024, tn=1024, tiles=(), colvecs=(), rowblocks=(), stats=False, name):
    m = a.shape[0]
    kdim = a.shape[1] if kdim is None else kdim
    tm, tn = min(tm, m), min(tn, n_cols)
    ni, nj = m // tm, n_cols // tn
    ck = kdim // ni
    col0, row0 = col_off // tn, k_block * ni

    def tile_index(jj, i):
        return (_row_index(jj, i), _col_index(jj))

    in_specs = [
        pl.BlockSpec((tm, kdim), lambda jj, i: (_row_index(jj, i), k_block)),
        pl.BlockSpec((None, ck, tn), lambda jj, i: (layer, row0 + i, col0 + jnp.minimum(jj, nj - 1))),
    ]
    in_specs += [pl.BlockSpec((tm, tn), tile_index) for _ in tiles]
    in_specs += [pl.BlockSpec((1, tn), lambda jj, i: (0, _col_index(jj))) for _ in colvecs]
    in_specs += [pl.BlockSpec((tm, rb.shape[1]), lambda jj, i: (_row_index(jj, i), 0)) for rb in rowblocks]
    out_specs = [pl.BlockSpec((tm, tn), tile_index) for _ in out_dtypes]
    out_shape = [jax.ShapeDtypeStruct((m, n_cols), dt) for dt in out_dtypes]
    if stats:
        out_specs.append(pl.BlockSpec((tm, LANES), tile_index))
        out_shape.append(jax.ShapeDtypeStruct((m, nj * LANES), F32))
    outs = pl.pallas_call(
        functools.partial(_wmm_kernel, epilogue=epilogue,
                          n_in=len(tiles) + len(colvecs) + len(rowblocks)),
        grid=(nj + 1, ni),
        in_specs=in_specs,
        out_specs=out_specs,
        out_shape=out_shape,
        scratch_shapes=[pltpu.VMEM((2, kdim, tn), BF16)],
        compiler_params=_params("arbitrary", "arbitrary"),
        name=name,
    )(a, w, *tiles, *colvecs, *rowblocks)
    return outs[0] if len(outs) == 1 else outs


def _merge_kernel(h_ref, a_ref, b_ref, cga_ref, cgb_ref, cba_ref, cbb_ref, o_ref,
                  wga_ref, wgb_ref, wba_ref, wbb_ref):
    jj = pl.program_id(0)
    staged = ((cga_ref, wga_ref), (cgb_ref, wgb_ref), (cba_ref, wba_ref), (cbb_ref, wbb_ref))

    @pl.when(jj == 0)
    def _():
        for c_ref, w_ref in staged:
            _stage_chunk(c_ref, w_ref)

    @pl.when(jj > 0)
    def _():
        h = h_ref[...]
        ga = _sigmoid(jnp.dot(h, _staged_weight(wga_ref), preferred_element_type=F32))
        out = ga * jnp.dot(a_ref[...], _staged_weight(wba_ref), preferred_element_type=F32)
        gb = _sigmoid(jnp.dot(h, _staged_weight(wgb_ref), preferred_element_type=F32))
        out = out + gb * jnp.dot(b_ref[...], _staged_weight(wbb_ref), preferred_element_type=F32)
        o_ref[...] = out.astype(o_ref.dtype)
        for c_ref, w_ref in staged:
            _stage_chunk(c_ref, w_ref)


def _merge(h, a, b, w_gate, w_branch_pool, w_branch_attn, layer, *, tm=1024, tn=512):
    m, d = h.shape
    tm, tn = min(tm, m), min(tn, d)
    ni, nj = m // tm, d // tn

    def rows(jj, i):
        return (_row_index(jj, i), 0)

    def chunk(col0):
        return lambda jj, i: (layer, i, col0 + jnp.minimum(jj, nj - 1))

    return pl.pallas_call(
        _merge_kernel,
        grid=(nj + 1, ni),
        in_specs=[
            pl.BlockSpec((tm, d), rows),
            pl.BlockSpec((tm, POOL_WIDTH), rows),
            pl.BlockSpec((tm, ATTN_WIDTH), rows),
            pl.BlockSpec((None, d // ni, tn), chunk(0)),
            pl.BlockSpec((None, d // ni, tn), chunk(nj)),
            pl.BlockSpec((None, POOL_WIDTH // ni, tn), chunk(0)),
            pl.BlockSpec((None, ATTN_WIDTH // ni, tn), chunk(0)),
        ],
        out_specs=pl.BlockSpec((tm, tn), lambda jj, i: (_row_index(jj, i), _col_index(jj))),
        out_shape=jax.ShapeDtypeStruct((m, d), BF16),
        scratch_shapes=[pltpu.VMEM((2, d, tn), BF16), pltpu.VMEM((2, d, tn), BF16),
                        pltpu.VMEM((2, POOL_WIDTH, tn), BF16), pltpu.VMEM((2, ATTN_WIDTH, tn), BF16)],
        compiler_params=_params("arbitrary", "arbitrary"),
        name="gated_merge",
    )(h, a, b, w_gate, w_gate, w_branch_pool, w_branch_attn)


def _pool_kernel(prev_ref, u_ref, next_ref, wp_ref, ps_ref, o_ref, buf_ref, *, seq):
    i = pl.program_id(0)
    tm = u_ref.shape[0]
    buf_ref[0:POOL_HALO, :] = jnp.where(i > 0, prev_ref[...], 0.0)
    buf_ref[POOL_HALO:POOL_HALO + tm, :] = u_ref[...]
    buf_ref[POOL_HALO + tm:, :] = jnp.where(i < pl.num_programs(0) - 1, next_ref[...], 0.0)
    t = i * tm + lax.broadcasted_iota(jnp.int32, (tm, 1), 0)
    for g, w in enumerate(POOL_WINDOWS):
        cols = slice(g * POOL_GROUP_DIM, (g + 1) * POOL_GROUP_DIM)
        total = None
        for o in range(-(w // 2), w // 2):
            piece = buf_ref[POOL_HALO + o:POOL_HALO + o + tm, cols]
            total = piece if total is None else total + piece
        cnt = (jnp.minimum(t + w // 2, seq) - jnp.maximum(t - w // 2, 0)).astype(F32)
        d = total / cnt - u_ref[:, cols]
        y = jnp.dot(d.astype(BF16), wp_ref[g], preferred_element_type=F32)
        o_ref[:, cols] = (y * ps_ref[:, cols]).astype(o_ref.dtype)


def _pool_mixer(u, w_pool, pool_scale, *, tm=512):
    s, c = u.shape
    halo_blocks = tm // POOL_HALO
    return pl.pallas_call(
        functools.partial(_pool_kernel, seq=s),
        grid=(s // tm,),
        in_specs=[
            pl.BlockSpec((POOL_HALO, c), lambda i: (jnp.maximum(i * halo_blocks - 1, 0), 0)),
            pl.BlockSpec((tm, c), lambda i: (i, 0)),
            pl.BlockSpec((POOL_HALO, c),
                         lambda i: (jnp.minimum((i + 1) * halo_blocks, s // POOL_HALO - 1), 0)),
            pl.BlockSpec((POOL_GROUPS, POOL_GROUP_DIM, POOL_GROUP_DIM), lambda i: (0, 0, 0)),
            pl.BlockSpec((1, c), lambda i: (0, 0)),
        ],
        out_specs=pl.BlockSpec((tm, c), lambda i: (i, 0)),
        out_shape=jax.ShapeDtypeStruct((s, c), BF16),
        scratch_shapes=[pltpu.VMEM((tm + 2 * POOL_HALO, c), F32)],
        compiler_params=_params("parallel"),
        name="pool_mixer",
    )(u, u, u, w_pool, pool_scale.reshape(1, c))


def _first_key_row(r, rows):
    return jnp.clip(r - WIN_ROWS // 2, 0, rows - WIN_ROWS)


def _span_start(blk, rows):
    return jnp.clip(blk * ATTN_ROWS - WIN_ROWS // 2, 0, rows - ATTN_SPAN)


def _attn_kernel(q_ref, k_ref, v_ref, bias_ref, o_ref, *, rows):
    blk = pl.program_id(0)
    span0 = _span_start(blk, rows)
    lane = lax.broadcasted_iota(jnp.int32, (GRID_W, LANES), 1)
    low = lane < HEAD_DIM
    n_keys = WIN_ROWS * GRID_W

    def query_rows(it, carry):
        geometry = []
        for sub in range(ROWS_PER_ITER):
            a = it * ROWS_PER_ITER + sub
            r = blk * ATTN_ROWS + a
            r0 = _first_key_row(r, rows)
            geometry.append((pl.ds(pl.multiple_of(a * GRID_W, GRID_W), GRID_W), r0 - span0, r - r0))

        def scores(sub, h):
            q_rows, off, variant = geometry[sub]
            pair, half = divmod(h, 2)
            cols = slice(pair * LANES, (pair + 1) * LANES)
            qp = q_ref[q_rows, cols]
            qm = jnp.where(low if half == 0 else jnp.logical_not(low), qp, jnp.zeros_like(qp))
            k_all = k_ref[pl.ds(off, WIN_ROWS), :, cols].reshape(n_keys, LANES)
            s = lax.dot_general(qm, k_all, (((1,), (1,)), ((), ())), preferred_element_type=F32)
            bias = jnp.concatenate(
                [bias_ref[h * BIAS_PAIRS + 2 * m + (WIN_ROWS - 1) - variant] for m in range(ROW_PAIRS)],
                axis=-1)
            return s * (HEAD_DIM ** -0.5 * LOG2E) + bias

        def weights(s):
            e = jnp.exp2(s - jnp.max(s, axis=-1, keepdims=True))
            return e.astype(BF16), jnp.sum(e, axis=-1, keepdims=True)

        def context(sub, h, e, denom):
            _, off, _ = geometry[sub]
            cols = slice((h // 2) * LANES, (h // 2 + 1) * LANES)
            v_all = v_ref[pl.ds(off, WIN_ROWS), :, cols].reshape(n_keys, LANES)
            return jnp.dot(e, v_all, preferred_element_type=F32) / denom

        chains = [(sub, h) for sub in range(ROWS_PER_ITER) for h in range(N_HEADS)]
        pending = {c: scores(*c) for c in chains[:HEAD_SKEW]}
        outs = {}
        for n, (sub, h) in enumerate(chains):
            e, denom = weights(pending.pop((sub, h)))
            if n + HEAD_SKEW < len(chains):
                ahead = chains[n + HEAD_SKEW]
                pending[ahead] = scores(*ahead)
            outs[h] = context(sub, h, e, denom)
            if h % 2 == 1:
                cols = slice((h // 2) * LANES, (h // 2 + 1) * LANES)
                o_ref[geometry[sub][0], cols] = jnp.where(
                    low, outs.pop(h - 1), outs.pop(h)).astype(o_ref.dtype)
        return carry

    lax.fori_loop(0, ATTN_ROWS // ROWS_PER_ITER, query_rows, 0)


def _attention_bias(rpb):
    depth = rpb.shape[0]
    n_dc = 2 * WIN_COLS - 1
    col = jnp.arange(GRID_W)
    c0 = jnp.clip(col - WIN_COLS // 2, 0, GRID_W - WIN_COLS)
    in_win = (col[None, :] >= c0[:, None]) & (col[None, :] < c0[:, None] + WIN_COLS)
    dc = col[None, :] - col[:, None] + (WIN_COLS - 1)
    onehot = (jnp.arange(n_dc)[:, None, None] == dc[None]).astype(F32).reshape(n_dc, GRID_W * GRID_W)
    tab = jnp.dot(rpb.reshape(-1, n_dc), onehot, precision=lax.Precision.HIGHEST)
    tab = tab.reshape(depth, N_HEADS, 2 * WIN_ROWS - 1, GRID_W, GRID_W)
    tab = jnp.where(in_win, tab * LOG2E, MASK_BIAS)
    two = jnp.concatenate([tab[:, :, :-1], tab[:, :, 1:]], axis=-1)
    return two.reshape(depth, N_HEADS * BIAS_PAIRS, GRID_W, 2 * GRID_W)


def _attention(qk, v, bias, layer):
    s = v.shape[0]
    rows = s // GRID_W
    assert rows >= ATTN_SPAN and rows % ATTN_ROWS == 0
    qk3 = qk.reshape(rows, GRID_W, 2 * ATTN_WIDTH)
    v3 = v.reshape(rows, GRID_W, ATTN_WIDTH)
    tq = ATTN_ROWS * GRID_W
    span_block = (pl.Element(ATTN_SPAN), pl.Element(GRID_W), pl.Element(ATTN_WIDTH))
    return pl.pallas_call(
        functools.partial(_attn_kernel, rows=rows),
        grid=(rows // ATTN_ROWS,),
        in_specs=[
            pl.BlockSpec((tq, ATTN_WIDTH), lambda b: (b, 0)),
            pl.BlockSpec(span_block, lambda b: (_span_start(b, rows), 0, ATTN_WIDTH)),
            pl.BlockSpec(span_block, lambda b: (_span_start(b, rows), 0, 0)),
            pl.BlockSpec((None, N_HEADS * BIAS_PAIRS, GRID_W, 2 * GRID_W),
                         lambda b: (layer, 0, 0, 0), pipeline_mode=pl.Buffered(1)),
        ],
        out_specs=pl.BlockSpec((tq, ATTN_WIDTH), lambda b: (b, 0)),
        out_shape=jax.ShapeDtypeStruct((s, ATTN_WIDTH), BF16),
        compiler_params=_params("parallel"),
        name="nbr_attention",
    )(qk, qk3, v3, bias)


def _ple_kernel(x_ref, p_ref, g_ref, wd_ref, wu_ref, wp_ref, *rest):
    emit_next = len(rest) == 3
    if emit_next:
        gn_ref, o_ref, hn_ref = rest
    else:
        (o_ref,) = rest
    subs = [pl.ds(k * PLE_SUB_ROWS, PLE_SUB_ROWS) for k in range(x_ref.shape[0] // PLE_SUB_ROWS)]
    hs = [_rmsnorm_rows(x_ref[s, :], g_ref[...]).astype(BF16) for s in subs]
    ts = [jnp.dot(h, wd_ref[...], preferred_element_type=F32).astype(BF16) for h in hs]
    gates = [_sigmoid(jnp.dot(t, wu_ref[...], preferred_element_type=F32)) for t in ts]
    for s, gate in zip(subs, gates):
        proj = jnp.dot(p_ref[s, :].astype(BF16), wp_ref[...], preferred_element_type=F32)
        y = x_ref[s, :] + gate * proj
        o_ref[s, :] = y
        if emit_next:
            hn_ref[s, :] = _rmsnorm_rows(y, gn_ref[...]).astype(hn_ref.dtype)


def _ple(x, p, g, wd, wu, wp, g_next=None, *, tm=256):
    s, d = x.shape
    r = wd.shape[1]
    row = lambda i: (i, 0)
    fixed = lambda i: (0, 0)
    in_specs = [
        pl.BlockSpec((tm, d), row),
        pl.BlockSpec((tm, PLE_DIM), row),
        pl.BlockSpec((1, d), fixed),
        pl.BlockSpec((d, r), fixed),
        pl.BlockSpec((r, d), fixed),
        pl.BlockSpec((PLE_DIM, d), fixed),
    ]
    args = [x, p, g.reshape(1, d), wd, wu, wp]
    out_specs = pl.BlockSpec((tm, d), row)
    out_shape = jax.ShapeDtypeStruct((s, d), F32)
    if g_next is not None:
        in_specs.append(pl.BlockSpec((1, d), fixed))
        args.append(g_next.reshape(1, d))
        out_specs = (out_specs, pl.BlockSpec((tm, d), row))
        out_shape = (out_shape, jax.ShapeDtypeStruct((s, d), BF16))
    return pl.pallas_call(
        _ple_kernel,
        grid=(s // tm,),
        in_specs=in_specs,
        out_specs=out_specs,
        out_shape=out_shape,
        compiler_params=_params("parallel"),
        name="ple",
    )(*args)


def kernel(x, p, norm_mix, w_in, w_pool, pool_scale, q_norm, k_norm, rpb,
           w_branch_pool, w_branch_attn, w_gate, w_out, norm_mlp, w_up, w_down,
           norm_ple, w_ple_gate_down, w_ple_gate_up, w_ple_proj):
    batch, seq, d = x.shape
    assert batch == 1
    depth = w_in.shape[0]
    d_ff = w_up.shape[2]
    xs = x.reshape(seq, d)
    heads_per_block = ATTN_WIDTH // HEAD_DIM
    attn_bias = _attention_bias(rpb)
    h = _rmsnorm(xs, norm_mix[0])
    for i in range(depth):
        qk_gain = jnp.concatenate([jnp.tile(q_norm[i], heads_per_block),
                                   jnp.tile(k_norm[i], heads_per_block)]).reshape(1, 2 * ATTN_WIDTH)
        u = _weight_matmul(_epi_store, h, w_in, i, n_cols=POOL_WIDTH, out_dtypes=[F32],
                           name="in_proj_pool")
        qk = _weight_matmul(_epi_headnorm, h, w_in, i, n_cols=2 * ATTN_WIDTH, col_off=POOL_WIDTH,
                            out_dtypes=[BF16], colvecs=[qk_gain], name="in_proj_qk")
        v = _weight_matmul(_epi_store, h, w_in, i, n_cols=ATTN_WIDTH,
                           col_off=POOL_WIDTH + 2 * ATTN_WIDTH, out_dtypes=[BF16], name="in_proj_v")
        a = _pool_mixer(u, w_pool[i].astype(BF16), pool_scale[i])
        b = _attention(qk, v, attn_bias, i)
        merged = _merge(h, a, b, w_gate, w_branch_pool, w_branch_attn, i)
        xs, hq, sumsq = _weight_matmul(
            _epi_residual_prenorm, merged, w_out, i, n_cols=d, out_dtypes=[F32, BF16], tm=512,
            tiles=[xs], colvecs=[norm_mlp[i].reshape(1, d)], stats=True, name="out_proj")
        up = _weight_matmul(functools.partial(_epi_rowscale_relu2, width=d), hq, w_up, i,
                            n_cols=d_ff, out_dtypes=[BF16], rowblocks=[sumsq], name="mlp_up")
        for kb in range(d_ff // d):
            xs = _weight_matmul(_epi_residual, up, w_down, i, n_cols=d, k_block=kb, kdim=d,
                                out_dtypes=[F32], tiles=[xs], name="mlp_down")

        ple_w = (w_ple_gate_down[i].astype(BF16), w_ple_gate_up[i].astype(BF16),
                 w_ple_proj[i].astype(BF16))
        p_i = p[i].reshape(seq, PLE_DIM)
        if i + 1 < depth:
            xs, h = _ple(xs, p_i, norm_ple[i], *ple_w, g_next=norm_mix[i + 1])
        else:
            xs = _ple(xs, p_i, norm_ple[i], *ple_w)
    return xs.reshape(batch, seq, d)
```

```python
import functools
import math

import jax
import jax.numpy as jnp
from jax import lax
from jax.experimental import pallas as pl
from jax.experimental.pallas import tpu as pltpu

POOL_GROUPS = 4
POOL_WIDTH = 1024
POOL_GROUP_DIM = 256
POOL_WINDOWS = (2, 4, 8, 16)
POOL_HALO = 8
N_HEADS = 16
HEAD_DIM = 64
ATTN_WIDTH = 1024
GRID_W = 64
WIN_ROWS = 8
WIN_COLS = 16
PLE_DIM = 256
PLE_SUB_ROWS = 128
EPS = 1e-6

LANES = 128
VMEM_LIMIT = 60 * 1024 * 1024
MASK_BIAS = -1e30
LOG2E = math.log2(math.e)
QUERY_SCALE = HEAD_DIM ** -0.5 * LOG2E

ATTN_ROWS = 16
ATTN_SPAN = ATTN_ROWS + WIN_ROWS - 1
ROWS_PER_ITER = 2
HEAD_SKEW = 8
ROW_PAIRS = WIN_ROWS // 2
BIAS_PAIRS = 2 * WIN_ROWS - 2

F32 = jnp.float32
BF16 = jnp.bfloat16


def _params(*sem):
    return pltpu.CompilerParams(dimension_semantics=sem, vmem_limit_bytes=VMEM_LIMIT)


def _sigmoid(x):
    return 1.0 / (1.0 + jnp.exp2(x * -LOG2E))


def _rmsnorm_rows(x, g):
    ms = jnp.mean(x * x, axis=-1, keepdims=True)
    return x * lax.rsqrt(ms + EPS) * g


def _rmsnorm_kernel(x_ref, g_ref, o_ref):
    o_ref[...] = _rmsnorm_rows(x_ref[...], g_ref[...]).astype(o_ref.dtype)


def _rmsnorm(x, g, *, tm=512):
    s, d = x.shape
    return pl.pallas_call(
        _rmsnorm_kernel,
        grid=(s // tm,),
        in_specs=[pl.BlockSpec((tm, d), lambda i: (i, 0)),
                  pl.BlockSpec((1, d), lambda i: (0, 0))],
        out_specs=pl.BlockSpec((tm, d), lambda i: (i, 0)),
        out_shape=jax.ShapeDtypeStruct((s, d), BF16),
        compiler_params=_params("parallel"),
        name="rmsnorm",
    )(x, g.reshape(1, d))


def _row_index(jj, i):
    return jnp.where(jj == 0, 0, i)


def _col_index(jj):
    return jnp.maximum(jj - 1, 0)


def _stage_chunk(wc_ref, wb_ref):
    jj, i = pl.program_id(0), pl.program_id(1)
    ck = wc_ref.shape[0]
    wb_ref[jj % 2, pl.ds(pl.multiple_of(i * ck, ck), ck), :] = wc_ref[...].astype(BF16)


def _staged_weight(wb_ref):
    return wb_ref[(pl.program_id(0) - 1) % 2]


def _head_rmsnorm(acc, g):
    lane = lax.broadcasted_iota(jnp.int32, (1, LANES), 1)
    low = lane < HEAD_DIM
    outs = []
    for c in range(acc.shape[1] // LANES):
        a = acc[:, c * LANES:(c + 1) * LANES]
        sq = a * a
        s_low = jnp.sum(jnp.where(low, sq, 0.0), axis=-1, keepdims=True)
        s_high = jnp.sum(jnp.where(low, 0.0, sq), axis=-1, keepdims=True)
        ms = jnp.where(low, s_low, s_high) * (1.0 / HEAD_DIM)
        outs.append(a * lax.rsqrt(ms + EPS) * g[:, c * LANES:(c + 1) * LANES])
    return jnp.concatenate(outs, axis=-1)


def _epi_store(acc, ins, outs):
    outs[0][...] = acc.astype(outs[0].dtype)


def _epi_headnorm(acc, ins, outs):
    outs[0][...] = _head_rmsnorm(acc, ins[0][...]).astype(outs[0].dtype)


def _epi_residual(acc, ins, outs):
    outs[0][...] = ins[0][...] + acc


def _epi_residual_prenorm(acc, ins, outs):
    x_ref, g_ref = ins
    o_ref, hq_ref, ss_ref = outs
    y = x_ref[...] + acc
    o_ref[...] = y
    hq_ref[...] = (y * g_ref[...]).astype(hq_ref.dtype)
    ss_ref[...] = jnp.broadcast_to(jnp.sum(y * y, axis=-1, keepdims=True), ss_ref.shape)


def _epi_rowscale_relu2(acc, ins, outs, *, width):
    ss = ins[0][...]
    total = ss[:, 0:1]
    for part in range(1, ss.shape[1] // LANES):
        total = total + ss[:, part * LANES:part * LANES + 1]
    y = jnp.maximum(acc * lax.rsqrt(total / width + EPS), 0.0)
    outs[0][...] = (y * y).astype(outs[0].dtype)


def _wmm_kernel(a_ref, wc_ref, *rest, epilogue, n_in):
    ins, outs, wb_ref = rest[:n_in], rest[n_in:-1], rest[-1]
    jj = pl.program_id(0)

    @pl.when(jj == 0)
    def _():
        _stage_chunk(wc_ref, wb_ref)

    @pl.when(jj > 0)
    def _():
        acc = jnp.dot(a_ref[...], _staged_weight(wb_ref), preferred_element_type=F32)
        epilogue(acc, ins, outs)
        _stage_chunk(wc_ref, wb_ref)


def _weight_matmul(epilogue, a, w, layer, *, n_cols, out_dtypes, col_off=0, k_passes=1,
                   tm=1024, tn=1024, tiles=(), colvecs=(), rowblocks=(), stats=False, name):
    m = a.shape[0]
    kdim = a.shape[1] // k_passes
    tm, tn = min(tm, m), min(tn, n_cols)
    ni, nj = m // tm, n_cols // tn
    ck = kdim // ni
    col0 = col_off // tn
    n_blocks = k_passes * nj

    def computed(jj):
        return jnp.maximum(jj - 1, 0)

    def staged(jj):
        return jnp.minimum(jj, n_blocks - 1)

    def tile_index(jj, i):
        return (_row_index(jj, i), computed(jj) % nj)

    in_specs = [
        pl.BlockSpec((tm, kdim), lambda jj, i: (_row_index(jj, i), computed(jj) // nj)),
        pl.BlockSpec((None, ck, tn),
                     lambda jj, i: (layer, (staged(jj) // nj) * ni + i, col0 + staged(jj) % nj)),
    ]
    in_specs += [pl.BlockSpec((tm, tn), tile_index) for _ in tiles]
    in_specs += [pl.BlockSpec((1, tn), lambda jj, i: (0, computed(jj) % nj)) for _ in colvecs]
    in_specs += [pl.BlockSpec((tm, rb.shape[1]), lambda jj, i: (_row_index(jj, i), 0)) for rb in rowblocks]
    out_specs = [pl.BlockSpec((tm, tn), tile_index) for _ in out_dtypes]
    out_shape = [jax.ShapeDtypeStruct((m, n_cols), dt) for dt in out_dtypes]
    if stats:
        out_specs.append(pl.BlockSpec((tm, LANES), tile_index))
        out_shape.append(jax.ShapeDtypeStruct((m, nj * LANES), F32))
    aliases = {}
    if k_passes > 1:
        assert len(tiles) == 1 and tiles[0].dtype == out_dtypes[0]
        aliases = {2: 0}
    outs = pl.pallas_call(
        functools.partial(_wmm_kernel, epilogue=epilogue,
                          n_in=len(tiles) + len(colvecs) + len(rowblocks)),
        grid=(n_blocks + 1, ni),
        in_specs=in_specs,
        out_specs=out_specs,
        out_shape=out_shape,
        scratch_shapes=[pltpu.VMEM((2, kdim, tn), BF16)],
        input_output_aliases=aliases,
        compiler_params=_params("arbitrary", "arbitrary"),
        name=name,
    )(a, w, *tiles, *colvecs, *rowblocks)
    return outs[0] if len(outs) == 1 else outs


def _merge_kernel(h_ref, a_ref, b_ref, cga_ref, cgb_ref, cba_ref, cbb_ref, o_ref,
                  wga_ref, wgb_ref, wba_ref, wbb_ref):
    jj = pl.program_id(0)
    staged = ((cga_ref, wga_ref), (cgb_ref, wgb_ref), (cba_ref, wba_ref), (cbb_ref, wbb_ref))

    @pl.when(jj == 0)
    def _():
        for c_ref, w_ref in staged:
            _stage_chunk(c_ref, w_ref)

    @pl.when(jj > 0)
    def _():
        h = h_ref[...]
        ga = _sigmoid(jnp.dot(h, _staged_weight(wga_ref), preferred_element_type=F32))
        out = ga * jnp.dot(a_ref[...], _staged_weight(wba_ref), preferred_element_type=F32)
        gb = _sigmoid(jnp.dot(h, _staged_weight(wgb_ref), preferred_element_type=F32))
        out = out + gb * jnp.dot(b_ref[...], _staged_weight(wbb_ref), preferred_element_type=F32)
        o_ref[...] = out.astype(o_ref.dtype)
        for c_ref, w_ref in staged:
            _stage_chunk(c_ref, w_ref)


def _merge(h, a, b, w_gate, w_branch_pool, w_branch_attn, layer, *, tm=1024, tn=512):
    m, d = h.shape
    tm, tn = min(tm, m), min(tn, d)
    ni, nj = m // tm, d // tn

    def rows(jj, i):
        return (_row_index(jj, i), 0)

    def chunk(col0):
        return lambda jj, i: (layer, i, col0 + jnp.minimum(jj, nj - 1))

    return pl.pallas_call(
        _merge_kernel,
        grid=(nj + 1, ni),
        in_specs=[
            pl.BlockSpec((tm, d), rows),
            pl.BlockSpec((tm, POOL_WIDTH), rows),
            pl.BlockSpec((tm, ATTN_WIDTH), rows),
            pl.BlockSpec((None, d // ni, tn), chunk(0)),
            pl.BlockSpec((None, d // ni, tn), chunk(nj)),
            pl.BlockSpec((None, POOL_WIDTH // ni, tn), chunk(0)),
            pl.BlockSpec((None, ATTN_WIDTH // ni, tn), chunk(0)),
        ],
        out_specs=pl.BlockSpec((tm, tn), lambda jj, i: (_row_index(jj, i), _col_index(jj))),
        out_shape=jax.ShapeDtypeStruct((m, d), BF16),
        scratch_shapes=[pltpu.VMEM((2, d, tn), BF16), pltpu.VMEM((2, d, tn), BF16),
                        pltpu.VMEM((2, POOL_WIDTH, tn), BF16), pltpu.VMEM((2, ATTN_WIDTH, tn), BF16)],
        compiler_params=_params("arbitrary", "arbitrary"),
        name="gated_merge",
    )(h, a, b, w_gate, w_gate, w_branch_pool, w_branch_attn)


def _pool_rows(prev_ref, u_ref, next_ref, wp_ref, ps_ref, o_ref, buf_ref, *, block, n_blocks, seq):
    tm = u_ref.shape[0]
    buf_ref[0:POOL_HALO, :] = jnp.where(block > 0, prev_ref[...], 0.0)
    buf_ref[POOL_HALO:POOL_HALO + tm, :] = u_ref[...]
    buf_ref[POOL_HALO + tm:, :] = jnp.where(block < n_blocks - 1, next_ref[...], 0.0)
    t = block * tm + lax.broadcasted_iota(jnp.int32, (tm, 1), 0)
    for g, w in enumerate(POOL_WINDOWS):
        cols = slice(g * POOL_GROUP_DIM, (g + 1) * POOL_GROUP_DIM)
        total = None
        for o in range(-(w // 2), w // 2):
            piece = buf_ref[POOL_HALO + o:POOL_HALO + o + tm, cols]
            total = piece if total is None else total + piece
        cnt = (jnp.minimum(t + w // 2, seq) - jnp.maximum(t - w // 2, 0)).astype(F32)
        d = total / cnt - u_ref[:, cols]
        y = jnp.dot(d.astype(BF16), wp_ref[g], preferred_element_type=F32)
        o_ref[:, cols] = (y * ps_ref[:, cols]).astype(o_ref.dtype)


def _pool_kernel(prev_ref, u_ref, next_ref, wp_ref, ps_ref, o_ref, buf_ref, *, seq):
    _pool_rows(prev_ref, u_ref, next_ref, wp_ref, ps_ref, o_ref, buf_ref,
               block=pl.program_id(0), n_blocks=pl.num_programs(0), seq=seq)


def _pool_mixer(u, w_pool, pool_scale, *, tm=512):
    s, c = u.shape
    halo_blocks = tm // POOL_HALO
    return pl.pallas_call(
        functools.partial(_pool_kernel, seq=s),
        grid=(s // tm,),
        in_specs=[
            pl.BlockSpec((POOL_HALO, c), lambda i: (jnp.maximum(i * halo_blocks - 1, 0), 0)),
            pl.BlockSpec((tm, c), lambda i: (i, 0)),
            pl.BlockSpec((POOL_HALO, c),
                         lambda i: (jnp.minimum((i + 1) * halo_blocks, s // POOL_HALO - 1), 0)),
            pl.BlockSpec((POOL_GROUPS, POOL_GROUP_DIM, POOL_GROUP_DIM), lambda i: (0, 0, 0)),
            pl.BlockSpec((1, c), lambda i: (0, 0)),
        ],
        out_specs=pl.BlockSpec((tm, c), lambda i: (i, 0)),
        out_shape=jax.ShapeDtypeStruct((s, c), BF16),
        scratch_shapes=[pltpu.VMEM((tm + 2 * POOL_HALO, c), F32)],
        compiler_params=_params("parallel"),
        name="pool_mixer",
    )(u, u, u, w_pool, pool_scale.reshape(1, c))


def _first_key_row(r, rows):
    return jnp.clip(r - WIN_ROWS // 2, 0, rows - WIN_ROWS)


def _span_start(blk, rows):
    return jnp.clip(blk * ATTN_ROWS - WIN_ROWS // 2, 0, rows - ATTN_SPAN)


def _attn_kernel(q_ref, k_ref, v_ref, bias_ref, o_ref, *, rows):
    blk = pl.program_id(0)
    span0 = _span_start(blk, rows)
    lane = lax.broadcasted_iota(jnp.int32, (GRID_W, LANES), 1)
    low = lane < HEAD_DIM
    n_keys = WIN_ROWS * GRID_W

    def query_rows(it, carry):
        geometry = []
        for sub in range(ROWS_PER_ITER):
            a = it * ROWS_PER_ITER + sub
            r = blk * ATTN_ROWS + a
            r0 = _first_key_row(r, rows)
            geometry.append((pl.ds(pl.multiple_of(a * GRID_W, GRID_W), GRID_W), r0 - span0, r - r0))

        def scores(sub, h):
            q_rows, off, variant = geometry[sub]
            pair, half = divmod(h, 2)
            cols = slice(pair * LANES, (pair + 1) * LANES)
            qp = q_ref[q_rows, cols]
            qm = jnp.where(low if half == 0 else jnp.logical_not(low), qp, jnp.zeros_like(qp))
            k_all = k_ref[pl.ds(off, WIN_ROWS), :, cols].reshape(n_keys, LANES)
            s = lax.dot_general(qm, k_all, (((1,), (1,)), ((), ())), preferred_element_type=F32)
            bias = jnp.concatenate(
                [bias_ref[h * BIAS_PAIRS + 2 * m + (WIN_ROWS - 1) - variant] for m in range(ROW_PAIRS)],
                axis=-1)
            return s + bias

        def weights(s):
            e = jnp.exp2(s - jnp.max(s, axis=-1, keepdims=True))
            return e.astype(BF16), jnp.sum(e, axis=-1, keepdims=True)

        def context(sub, h, e, denom):
            _, off, _ = geometry[sub]
            cols = slice((h // 2) * LANES, (h // 2 + 1) * LANES)
            v_all = v_ref[pl.ds(off, WIN_ROWS), :, cols].reshape(n_keys, LANES)
            return jnp.dot(e, v_all, preferred_element_type=F32) / denom

        chains = [(sub, h) for sub in range(ROWS_PER_ITER) for h in range(N_HEADS)]
        pending = {c: scores(*c) for c in chains[:HEAD_SKEW]}
        outs = {}
        for n, (sub, h) in enumerate(chains):
            e, denom = weights(pending.pop((sub, h)))
            if n + HEAD_SKEW < len(chains):
                ahead = chains[n + HEAD_SKEW]
                pending[ahead] = scores(*ahead)
            outs[h] = context(sub, h, e, denom)
            if h % 2 == 1:
                cols = slice((h // 2) * LANES, (h // 2 + 1) * LANES)
                o_ref[geometry[sub][0], cols] = jnp.where(
                    low, outs.pop(h - 1), outs.pop(h)).astype(o_ref.dtype)
        return carry

    lax.fori_loop(0, ATTN_ROWS // ROWS_PER_ITER, query_rows, 0)


def _attention_bias(rpb):
    depth = rpb.shape[0]
    n_dc = 2 * WIN_COLS - 1
    col = jnp.arange(GRID_W)
    c0 = jnp.clip(col - WIN_COLS // 2, 0, GRID_W - WIN_COLS)
    in_win = (col[None, :] >= c0[:, None]) & (col[None, :] < c0[:, None] + WIN_COLS)
    dc = col[None, :] - col[:, None] + (WIN_COLS - 1)
    onehot = (jnp.arange(n_dc)[:, None, None] == dc[None]).astype(F32).reshape(n_dc, GRID_W * GRID_W)
    tab = jnp.dot(rpb.reshape(-1, n_dc), onehot, precision=lax.Precision.HIGHEST)
    tab = tab.reshape(depth, N_HEADS, 2 * WIN_ROWS - 1, GRID_W, GRID_W)
    tab = jnp.where(in_win, tab * LOG2E, MASK_BIAS)
    two = jnp.concatenate([tab[:, :, :-1], tab[:, :, 1:]], axis=-1)
    return two.reshape(depth, N_HEADS * BIAS_PAIRS, GRID_W, 2 * GRID_W)


def _attention(qk, v, bias, layer):
    s = v.shape[0]
    rows = s // GRID_W
    assert rows >= ATTN_SPAN and rows % ATTN_ROWS == 0
    qk3 = qk.reshape(rows, GRID_W, 2 * ATTN_WIDTH)
    v3 = v.reshape(rows, GRID_W, ATTN_WIDTH)
    tq = ATTN_ROWS * GRID_W
    span_block = (pl.Element(ATTN_SPAN), pl.Element(GRID_W), pl.Element(ATTN_WIDTH))
    return pl.pallas_call(
        functools.partial(_attn_kernel, rows=rows),
        grid=(rows // ATTN_ROWS,),
        in_specs=[
            pl.BlockSpec((tq, ATTN_WIDTH), lambda b: (b, 0)),
            pl.BlockSpec(span_block, lambda b: (_span_start(b, rows), 0, ATTN_WIDTH)),
            pl.BlockSpec(span_block, lambda b: (_span_start(b, rows), 0, 0)),
            pl.BlockSpec((None, N_HEADS * BIAS_PAIRS, GRID_W, 2 * GRID_W),
                         lambda b: (layer, 0, 0, 0), pipeline_mode=pl.Buffered(1)),
        ],
        out_specs=pl.BlockSpec((tq, ATTN_WIDTH), lambda b: (b, 0)),
        out_shape=jax.ShapeDtypeStruct((s, ATTN_WIDTH), BF16),
        compiler_params=_params("parallel"),
        name="nbr_attention",
    )(qk, qk3, v3, bias)


def _ple_kernel(x_ref, p_ref, g_ref, wd_ref, wu_ref, wp_ref, *rest):
    emit_next = len(rest) == 3
    if emit_next:
        gn_ref, o_ref, hn_ref = rest
    else:
        (o_ref,) = rest
    subs = [pl.ds(k * PLE_SUB_ROWS, PLE_SUB_ROWS) for k in range(x_ref.shape[0] // PLE_SUB_ROWS)]
    hs = [_rmsnorm_rows(x_ref[s, :], g_ref[...]).astype(BF16) for s in subs]
    ts = [jnp.dot(h, wd_ref[...], preferred_element_type=F32).astype(BF16) for h in hs]
    gates = [_sigmoid(jnp.dot(t, wu_ref[...], preferred_element_type=F32)) for t in ts]
    for s, gate in zip(subs, gates):
        proj = jnp.dot(p_ref[s, :].astype(BF16), wp_ref[...], preferred_element_type=F32)
        y = x_ref[s, :] + gate * proj
        o_ref[s, :] = y
        if emit_next:
            hn_ref[s, :] = _rmsnorm_rows(y, gn_ref[...]).astype(hn_ref.dtype)


def _ple(x, p, g, wd, wu, wp, g_next=None, *, tm=256):
    s, d = x.shape
    r = wd.shape[1]
    row = lambda i: (i, 0)
    fixed = lambda i: (0, 0)
    in_specs = [
        pl.BlockSpec((tm, d), row),
        pl.BlockSpec((tm, PLE_DIM), row),
        pl.BlockSpec((1, d), fixed),
        pl.BlockSpec((d, r), fixed),
        pl.BlockSpec((r, d), fixed),
        pl.BlockSpec((PLE_DIM, d), fixed),
    ]
    args = [x, p, g.reshape(1, d), wd, wu, wp]
    out_specs = pl.BlockSpec((tm, d), row)
    out_shape = jax.ShapeDtypeStruct((s, d), F32)
    if g_next is not None:
        in_specs.append(pl.BlockSpec((1, d), fixed))
        args.append(g_next.reshape(1, d))
        out_specs = (out_specs, pl.BlockSpec((tm, d), row))
        out_shape = (out_shape, jax.ShapeDtypeStruct((s, d), BF16))
    return pl.pallas_call(
        _ple_kernel,
        grid=(s // tm,),
        in_specs=in_specs,
        out_specs=out_specs,
        out_shape=out_shape,
        compiler_params=_params("parallel"),
        name="ple",
    )(*args)


def kernel(x, p, norm_mix, w_in, w_pool, pool_scale, q_norm, k_norm, rpb,
           w_branch_pool, w_branch_attn, w_gate, w_out, norm_mlp, w_up, w_down,
           norm_ple, w_ple_gate_down, w_ple_gate_up, w_ple_proj):
    batch, seq, d = x.shape
    assert batch == 1
    depth = w_in.shape[0]
    d_ff = w_up.shape[2]
    xs = x.reshape(seq, d)
    heads_per_block = ATTN_WIDTH // HEAD_DIM
    attn_bias = _attention_bias(rpb)
    h = _rmsnorm(xs, norm_mix[0])
    for i in range(depth):
        qk_gain = jnp.concatenate([jnp.tile(q_norm[i] * QUERY_SCALE, heads_per_block),
                                   jnp.tile(k_norm[i], heads_per_block)]).reshape(1, 2 * ATTN_WIDTH)
        u = _weight_matmul(_epi_store, h, w_in, i, n_cols=POOL_WIDTH, out_dtypes=[F32],
                           name="in_proj_pool")
        qk = _weight_matmul(_epi_headnorm, h, w_in, i, n_cols=2 * ATTN_WIDTH, col_off=POOL_WIDTH,
                            out_dtypes=[BF16], colvecs=[qk_gain], name="in_proj_qk")
        v = _weight_matmul(_epi_store, h, w_in, i, n_cols=ATTN_WIDTH,
                           col_off=POOL_WIDTH + 2 * ATTN_WIDTH, out_dtypes=[BF16], name="in_proj_v")
        a = _pool_mixer(u, w_pool[i].astype(BF16), pool_scale[i])
        b = _attention(qk, v, attn_bias, i)
        merged = _merge(h, a, b, w_gate, w_branch_pool, w_branch_attn, i)
        xs, hq, sumsq = _weight_matmul(
            _epi_residual_prenorm, merged, w_out, i, n_cols=d, out_dtypes=[F32, BF16], tm=512,
            tiles=[xs], colvecs=[norm_mlp[i].reshape(1, d)], stats=True, name="out_proj")
        up = _weight_matmul(functools.partial(_epi_rowscale_relu2, width=d), hq, w_up, i,
                            n_cols=d_ff, out_dtypes=[BF16], rowblocks=[sumsq], name="mlp_up")
        xs = _weight_matmul(_epi_residual, up, w_down, i, n_cols=d, k_passes=d_ff // d,
                            out_dtypes=[F32], tiles=[xs], name="mlp_down")

        ple_w = (w_ple_gate_down[i].astype(BF16), w_ple_gate_up[i].astype(BF16),
                 w_ple_proj[i].astype(BF16))
        p_i = p[i].reshape(seq, PLE_DIM)
        if i + 1 < depth:
            xs, h = _ple(xs, p_i, norm_ple[i], *ple_w, g_next=norm_mix[i + 1])
        else:
            xs = _ple(xs, p_i, norm_ple[i], *ple_w)
    return xs.reshape(batch, seq, d)
```

```python
import functools
import math

import jax
import jax.numpy as jnp
from jax import lax
from jax.experimental import pallas as pl
from jax.experimental.pallas import tpu as pltpu

POOL_GROUPS = 4
POOL_WIDTH = 1024
POOL_GROUP_DIM = 256
POOL_WINDOWS = (2, 4, 8, 16)
POOL_HALO = 8
N_HEADS = 16
HEAD_DIM = 64
ATTN_WIDTH = 1024
GRID_W = 64
WIN_ROWS = 8
WIN_COLS = 16
PLE_DIM = 256
PLE_SUB_ROWS = 128
EPS = 1e-6

LANES = 128
VMEM_LIMIT = 60 * 1024 * 1024
MASK_BIAS = -1e30
LOG2E = math.log2(math.e)
QUERY_SCALE = HEAD_DIM ** -0.5 * LOG2E

ATTN_ROWS = 16
ATTN_SPAN = ATTN_ROWS + WIN_ROWS - 1
ROWS_PER_ITER = 2
HEAD_SKEW = 8
ROW_PAIRS = WIN_ROWS // 2
BIAS_PAIRS = 2 * WIN_ROWS - 2

F32 = jnp.float32
BF16 = jnp.bfloat16


def _params(*sem):
    return pltpu.CompilerParams(dimension_semantics=sem, vmem_limit_bytes=VMEM_LIMIT)


def _sigmoid(x):
    return 1.0 / (1.0 + jnp.exp2(x * -LOG2E))


def _rmsnorm_rows(x, g):
    ms = jnp.mean(x * x, axis=-1, keepdims=True)
    return x * lax.rsqrt(ms + EPS) * g


def _rmsnorm_kernel(x_ref, g_ref, o_ref):
    o_ref[...] = _rmsnorm_rows(x_ref[...], g_ref[...]).astype(o_ref.dtype)


def _rmsnorm(x, g, *, tm=512):
    s, d = x.shape
    return pl.pallas_call(
        _rmsnorm_kernel,
        grid=(s // tm,),
        in_specs=[pl.BlockSpec((tm, d), lambda i: (i, 0)),
                  pl.BlockSpec((1, d), lambda i: (0, 0))],
        out_specs=pl.BlockSpec((tm, d), lambda i: (i, 0)),
        out_shape=jax.ShapeDtypeStruct((s, d), BF16),
        compiler_params=_params("parallel"),
        name="rmsnorm",
    )(x, g.reshape(1, d))


def _row_index(jj, i):
    return jnp.where(jj == 0, 0, i)


def _col_index(jj):
    return jnp.maximum(jj - 1, 0)


def _stage_chunk(wc_ref, wb_ref):
    jj, i = pl.program_id(0), pl.program_id(1)
    ck = wc_ref.shape[0]
    wb_ref[jj % 2, pl.ds(pl.multiple_of(i * ck, ck), ck), :] = wc_ref[...].astype(BF16)


def _staged_weight(wb_ref):
    return wb_ref[(pl.program_id(0) - 1) % 2]


def _head_rmsnorm(acc, g):
    lane = lax.broadcasted_iota(jnp.int32, (1, LANES), 1)
    low = lane < HEAD_DIM
    outs = []
    for c in range(acc.shape[1] // LANES):
        a = acc[:, c * LANES:(c + 1) * LANES]
        sq = a * a
        s_low = jnp.sum(jnp.where(low, sq, 0.0), axis=-1, keepdims=True)
        s_high = jnp.sum(jnp.where(low, 0.0, sq), axis=-1, keepdims=True)
        ms = jnp.where(low, s_low, s_high) * (1.0 / HEAD_DIM)
        outs.append(a * lax.rsqrt(ms + EPS) * g[:, c * LANES:(c + 1) * LANES])
    return jnp.concatenate(outs, axis=-1)


def _epi_residual(acc, ins, outs):
    outs[0][...] = ins[0][...] + acc


def _epi_residual_prenorm(acc, ins, outs):
    x_ref, g_ref = ins
    o_ref, hq_ref, ss_ref = outs
    y = x_ref[...] + acc
    o_ref[...] = y
    hq_ref[...] = (y * g_ref[...]).astype(hq_ref.dtype)
    ss_ref[...] = jnp.broadcast_to(jnp.sum(y * y, axis=-1, keepdims=True), ss_ref.shape)


def _epi_rowscale_relu2(acc, ins, outs, *, width):
    ss = ins[0][...]
    total = ss[:, 0:1]
    for part in range(1, ss.shape[1] // LANES):
        total = total + ss[:, part * LANES:part * LANES + 1]
    y = jnp.maximum(acc * lax.rsqrt(total / width + EPS), 0.0)
    outs[0][...] = (y * y).astype(outs[0].dtype)


def _wmm_kernel(a_ref, wc_ref, *rest, epilogue, n_in):
    ins, outs, wb_ref = rest[:n_in], rest[n_in:-1], rest[-1]
    jj = pl.program_id(0)

    @pl.when(jj == 0)
    def _():
        _stage_chunk(wc_ref, wb_ref)

    @pl.when(jj > 0)
    def _():
        acc = jnp.dot(a_ref[...], _staged_weight(wb_ref), preferred_element_type=F32)
        epilogue(acc, ins, outs)
        _stage_chunk(wc_ref, wb_ref)


def _weight_matmul(epilogue, a, w, layer, *, n_cols, out_dtypes, k_passes=1,
                   tm=1024, tn=1024, tiles=(), colvecs=(), rowblocks=(), stats=False, name):
    m = a.shape[0]
    kdim = a.shape[1] // k_passes
    tm, tn = min(tm, m), min(tn, n_cols)
    ni, nj = m // tm, n_cols // tn
    ck = kdim // ni
    n_blocks = k_passes * nj

    def computed(jj):
        return jnp.maximum(jj - 1, 0)

    def staged(jj):
        return jnp.minimum(jj, n_blocks - 1)

    def tile_index(jj, i):
        return (_row_index(jj, i), computed(jj) % nj)

    in_specs = [
        pl.BlockSpec((tm, kdim), lambda jj, i: (_row_index(jj, i), computed(jj) // nj)),
        pl.BlockSpec((None, ck, tn),
                     lambda jj, i: (layer, (staged(jj) // nj) * ni + i, staged(jj) % nj)),
    ]
    in_specs += [pl.BlockSpec((tm, tn), tile_index) for _ in tiles]
    in_specs += [pl.BlockSpec((1, tn), lambda jj, i: (0, computed(jj) % nj)) for _ in colvecs]
    in_specs += [pl.BlockSpec((tm, rb.shape[1]), lambda jj, i: (_row_index(jj, i), 0)) for rb in rowblocks]
    out_specs = [pl.BlockSpec((tm, tn), tile_index) for _ in out_dtypes]
    out_shape = [jax.ShapeDtypeStruct((m, n_cols), dt) for dt in out_dtypes]
    if stats:
        out_specs.append(pl.BlockSpec((tm, LANES), tile_index))
        out_shape.append(jax.ShapeDtypeStruct((m, nj * LANES), F32))
    aliases = {}
    if k_passes > 1:
        assert len(tiles) == 1 and tiles[0].dtype == out_dtypes[0]
        aliases = {2: 0}
    outs = pl.pallas_call(
        functools.partial(_wmm_kernel, epilogue=epilogue,
                          n_in=len(tiles) + len(colvecs) + len(rowblocks)),
        grid=(n_blocks + 1, ni),
        in_specs=in_specs,
        out_specs=out_specs,
        out_shape=out_shape,
        scratch_shapes=[pltpu.VMEM((2, kdim, tn), BF16)],
        input_output_aliases=aliases,
        compiler_params=_params("arbitrary", "arbitrary"),
        name=name,
    )(a, w, *tiles, *colvecs, *rowblocks)
    return outs[0] if len(outs) == 1 else outs


def _in_proj_kernel(h_ref, wc_ref, g_ref, u_ref, qkv_ref, wb_ref):
    jj = pl.program_id(0)

    def product():
        return jnp.dot(h_ref[...], _staged_weight(wb_ref), preferred_element_type=F32)

    @pl.when(jj == 0)
    def _():
        _stage_chunk(wc_ref, wb_ref)

    @pl.when(jj == 1)
    def _():
        u_ref[...] = product()
        _stage_chunk(wc_ref, wb_ref)

    @pl.when((jj == 2) | (jj == 3))
    def _():
        qkv_ref[...] = _head_rmsnorm(product(), g_ref[...]).astype(qkv_ref.dtype)
        _stage_chunk(wc_ref, wb_ref)

    @pl.when(jj == 4)
    def _():
        qkv_ref[...] = product().astype(qkv_ref.dtype)
        _stage_chunk(wc_ref, wb_ref)


def _in_proj(h, w_in, layer, qk_gain, *, tm=1024):
    m, kdim = h.shape
    tn = ATTN_WIDTH
    assert POOL_WIDTH == tn and w_in.shape[2] == 4 * tn
    tm = min(tm, m)
    ni = m // tm
    last_step = 4
    return pl.pallas_call(
        _in_proj_kernel,
        grid=(last_step + 1, ni),
        in_specs=[
            pl.BlockSpec((tm, kdim), lambda jj, i: (_row_index(jj, i), 0)),
            pl.BlockSpec((None, kdim // ni, tn), lambda jj, i: (layer, i, jnp.minimum(jj, 3))),
            pl.BlockSpec((1, tn), lambda jj, i: (0, jnp.clip(jj - 2, 0, 1))),
        ],
        out_specs=[
            pl.BlockSpec((tm, tn), lambda jj, i: (jnp.where(jj <= 1, _row_index(jj, i), ni - 1), 0)),
            pl.BlockSpec((tm, tn), lambda jj, i: (jnp.where(jj >= 2, i, 0), jnp.maximum(jj - 2, 0))),
        ],
        out_shape=[jax.ShapeDtypeStruct((m, tn), F32), jax.ShapeDtypeStruct((m, 3 * tn), BF16)],
        scratch_shapes=[pltpu.VMEM((2, kdim, tn), BF16)],
        compiler_params=_params("arbitrary", "arbitrary"),
        name="in_proj",
    )(h, w_in, qk_gain)


def _merge_kernel(h_ref, a_ref, b_ref, cga_ref, cgb_ref, cba_ref, cbb_ref, o_ref,
                  wga_ref, wgb_ref, wba_ref, wbb_ref):
    jj = pl.program_id(0)
    staged = ((cga_ref, wga_ref), (cgb_ref, wgb_ref), (cba_ref, wba_ref), (cbb_ref, wbb_ref))

    @pl.when(jj == 0)
    def _():
        for c_ref, w_ref in staged:
            _stage_chunk(c_ref, w_ref)

    @pl.when(jj > 0)
    def _():
        h = h_ref[...]
        ga = _sigmoid(jnp.dot(h, _staged_weight(wga_ref), preferred_element_type=F32))
        out = ga * jnp.dot(a_ref[...], _staged_weight(wba_ref), preferred_element_type=F32)
        gb = _sigmoid(jnp.dot(h, _staged_weight(wgb_ref), preferred_element_type=F32))
        out = out + gb * jnp.dot(b_ref[...], _staged_weight(wbb_ref), preferred_element_type=F32)
        o_ref[...] = out.astype(o_ref.dtype)
        for c_ref, w_ref in staged:
            _stage_chunk(c_ref, w_ref)


def _merge(h, a, b, w_gate, w_branch_pool, w_branch_attn, layer, *, tm=1024, tn=512):
    m, d = h.shape
    tm, tn = min(tm, m), min(tn, d)
    ni, nj = m // tm, d // tn

    def rows(jj, i):
        return (_row_index(jj, i), 0)

    def chunk(col0):
        return lambda jj, i: (layer, i, col0 + jnp.minimum(jj, nj - 1))

    return pl.pallas_call(
        _merge_kernel,
        grid=(nj + 1, ni),
        in_specs=[
            pl.BlockSpec((tm, d), rows),
            pl.BlockSpec((tm, POOL_WIDTH), rows),
            pl.BlockSpec((tm, ATTN_WIDTH), rows),
            pl.BlockSpec((None, d // ni, tn), chunk(0)),
            pl.BlockSpec((None, d // ni, tn), chunk(nj)),
            pl.BlockSpec((None, POOL_WIDTH // ni, tn), chunk(0)),
            pl.BlockSpec((None, ATTN_WIDTH // ni, tn), chunk(0)),
        ],
        out_specs=pl.BlockSpec((tm, tn), lambda jj, i: (_row_index(jj, i), _col_index(jj))),
        out_shape=jax.ShapeDtypeStruct((m, d), BF16),
        scratch_shapes=[pltpu.VMEM((2, d, tn), BF16), pltpu.VMEM((2, d, tn), BF16),
                        pltpu.VMEM((2, POOL_WIDTH, tn), BF16), pltpu.VMEM((2, ATTN_WIDTH, tn), BF16)],
        compiler_params=_params("arbitrary", "arbitrary"),
        name="gated_merge",
    )(h, a, b, w_gate, w_gate, w_branch_pool, w_branch_attn)


def _pool_rows(prev_ref, u_ref, next_ref, wp_ref, ps_ref, o_ref, buf_ref, *, block, n_blocks, seq):
    tm = u_ref.shape[0]
    buf_ref[0:POOL_HALO, :] = jnp.where(block > 0, prev_ref[...], 0.0)
    buf_ref[POOL_HALO:POOL_HALO + tm, :] = u_ref[...]
    buf_ref[POOL_HALO + tm:, :] = jnp.where(block < n_blocks - 1, next_ref[...], 0.0)
    t = block * tm + lax.broadcasted_iota(jnp.int32, (tm, 1), 0)
    for g, w in enumerate(POOL_WINDOWS):
        cols = slice(g * POOL_GROUP_DIM, (g + 1) * POOL_GROUP_DIM)
        total = None
        for o in range(-(w // 2), w // 2):
            piece = buf_ref[POOL_HALO + o:POOL_HALO + o + tm, cols]
            total = piece if total is None else total + piece
        cnt = (jnp.minimum(t + w // 2, seq) - jnp.maximum(t - w // 2, 0)).astype(F32)
        d = total / cnt - u_ref[:, cols]
        y = jnp.dot(d.astype(BF16), wp_ref[g], preferred_element_type=F32)
        o_ref[:, cols] = (y * ps_ref[:, cols]).astype(o_ref.dtype)


def _pool_kernel(prev_ref, u_ref, next_ref, wp_ref, ps_ref, o_ref, buf_ref, *, seq):
    _pool_rows(prev_ref, u_ref, next_ref, wp_ref, ps_ref, o_ref, buf_ref,
               block=pl.program_id(0), n_blocks=pl.num_programs(0), seq=seq)


def _pool_mixer(u, w_pool, pool_scale, *, tm=512):
    s, c = u.shape
    halo_blocks = tm // POOL_HALO
    return pl.pallas_call(
        functools.partial(_pool_kernel, seq=s),
        grid=(s // tm,),
        in_specs=[
            pl.BlockSpec((POOL_HALO, c), lambda i: (jnp.maximum(i * halo_blocks - 1, 0), 0)),
            pl.BlockSpec((tm, c), lambda i: (i, 0)),
            pl.BlockSpec((POOL_HALO, c),
                         lambda i: (jnp.minimum((i + 1) * halo_blocks, s // POOL_HALO - 1), 0)),
            pl.BlockSpec((POOL_GROUPS, POOL_GROUP_DIM, POOL_GROUP_DIM), lambda i: (0, 0, 0)),
            pl.BlockSpec((1, c), lambda i: (0, 0)),
        ],
        out_specs=pl.BlockSpec((tm, c), lambda i: (i, 0)),
        out_shape=jax.ShapeDtypeStruct((s, c), BF16),
        scratch_shapes=[pltpu.VMEM((tm + 2 * POOL_HALO, c), F32)],
        compiler_params=_params("parallel"),
        name="pool_mixer",
    )(u, u, u, w_pool, pool_scale.reshape(1, c))


def _first_key_row(r, rows):
    return jnp.clip(r - WIN_ROWS // 2, 0, rows - WIN_ROWS)


def _span_start(blk, rows):
    return jnp.clip(blk * ATTN_ROWS - WIN_ROWS // 2, 0, rows - ATTN_SPAN)


def _attn_kernel(q_ref, k_ref, v_ref, bias_ref, o_ref, *, rows):
    blk = pl.program_id(0)
    span0 = _span_start(blk, rows)
    lane = lax.broadcasted_iota(jnp.int32, (GRID_W, LANES), 1)
    low = lane < HEAD_DIM
    n_keys = WIN_ROWS * GRID_W

    def query_rows(it, carry):
        geometry = []
        for sub in range(ROWS_PER_ITER):
            a = it * ROWS_PER_ITER + sub
            r = blk * ATTN_ROWS + a
            r0 = _first_key_row(r, rows)
            geometry.append((pl.ds(pl.multiple_of(a * GRID_W, GRID_W), GRID_W), r0 - span0, r - r0))

        def scores(sub, h):
            q_rows, off, variant = geometry[sub]
            pair, half = divmod(h, 2)
            cols = slice(pair * LANES, (pair + 1) * LANES)
            qp = q_ref[q_rows, cols]
            qm = jnp.where(low if half == 0 else jnp.logical_not(low), qp, jnp.zeros_like(qp))
            k_all = k_ref[pl.ds(off, WIN_ROWS), :, cols].reshape(n_keys, LANES)
            s = lax.dot_general(qm, k_all, (((1,), (1,)), ((), ())), preferred_element_type=F32)
            bias = jnp.concatenate(
                [bias_ref[h * BIAS_PAIRS + 2 * m + (WIN_ROWS - 1) - variant] for m in range(ROW_PAIRS)],
                axis=-1)
            return s + bias

        def weights(s):
            e = jnp.exp2(s - jnp.max(s, axis=-1, keepdims=True))
            return e.astype(BF16), jnp.sum(e, axis=-1, keepdims=True)

        def context(sub, h, e, denom):
            _, off, _ = geometry[sub]
            cols = slice((h // 2) * LANES, (h // 2 + 1) * LANES)
            v_all = v_ref[pl.ds(off, WIN_ROWS), :, cols].reshape(n_keys, LANES)
            return jnp.dot(e, v_all, preferred_element_type=F32) / denom

        chains = [(sub, h) for sub in range(ROWS_PER_ITER) for h in range(N_HEADS)]
        pending = {c: scores(*c) for c in chains[:HEAD_SKEW]}
        outs = {}
        for n, (sub, h) in enumerate(chains):
            e, denom = weights(pending.pop((sub, h)))
            if n + HEAD_SKEW < len(chains):
                ahead = chains[n + HEAD_SKEW]
                pending[ahead] = scores(*ahead)
            outs[h] = context(sub, h, e, denom)
            if h % 2 == 1:
                cols = slice((h // 2) * LANES, (h // 2 + 1) * LANES)
                o_ref[geometry[sub][0], cols] = jnp.where(
                    low, outs.pop(h - 1), outs.pop(h)).astype(o_ref.dtype)
        return carry

    lax.fori_loop(0, ATTN_ROWS // ROWS_PER_ITER, query_rows, 0)


def _attention_bias(rpb):
    depth = rpb.shape[0]
    n_dc = 2 * WIN_COLS - 1
    col = jnp.arange(GRID_W)
    c0 = jnp.clip(col - WIN_COLS // 2, 0, GRID_W - WIN_COLS)
    in_win = (col[None, :] >= c0[:, None]) & (col[None, :] < c0[:, None] + WIN_COLS)
    dc = col[None, :] - col[:, None] + (WIN_COLS - 1)
    onehot = (jnp.arange(n_dc)[:, None, None] == dc[None]).astype(F32).reshape(n_dc, GRID_W * GRID_W)
    tab = jnp.dot(rpb.reshape(-1, n_dc), onehot, precision=lax.Precision.HIGHEST)
    tab = tab.reshape(depth, N_HEADS, 2 * WIN_ROWS - 1, GRID_W, GRID_W)
    tab = jnp.where(in_win, tab * LOG2E, MASK_BIAS)
    two = jnp.concatenate([tab[:, :, :-1], tab[:, :, 1:]], axis=-1)
    return two.reshape(depth, N_HEADS * BIAS_PAIRS, GRID_W, 2 * GRID_W)


def _attention(qkv, bias, layer):
    s = qkv.shape[0]
    rows = s // GRID_W
    assert rows >= ATTN_SPAN and rows % ATTN_ROWS == 0
    qkv3 = qkv.reshape(rows, GRID_W, 3 * ATTN_WIDTH)
    tq = ATTN_ROWS * GRID_W
    span_block = (pl.Element(ATTN_SPAN), pl.Element(GRID_W), pl.Element(ATTN_WIDTH))
    return pl.pallas_call(
        functools.partial(_attn_kernel, rows=rows),
        grid=(rows // ATTN_ROWS,),
        in_specs=[
            pl.BlockSpec((tq, ATTN_WIDTH), lambda b: (b, 0)),
            pl.BlockSpec(span_block, lambda b: (_span_start(b, rows), 0, ATTN_WIDTH)),
            pl.BlockSpec(span_block, lambda b: (_span_start(b, rows), 0, 2 * ATTN_WIDTH)),
            pl.BlockSpec((None, N_HEADS * BIAS_PAIRS, GRID_W, 2 * GRID_W),
                         lambda b: (layer, 0, 0, 0), pipeline_mode=pl.Buffered(1)),
        ],
        out_specs=pl.BlockSpec((tq, ATTN_WIDTH), lambda b: (b, 0)),
        out_shape=jax.ShapeDtypeStruct((s, ATTN_WIDTH), BF16),
        compiler_params=_params("parallel"),
        name="nbr_attention",
    )(qkv, qkv3, qkv3, bias)


def _ple_kernel(x_ref, p_ref, g_ref, wd_ref, wu_ref, wp_ref, *rest):
    emit_next = len(rest) == 3
    if emit_next:
        gn_ref, o_ref, hn_ref = rest
    else:
        (o_ref,) = rest
    subs = [pl.ds(k * PLE_SUB_ROWS, PLE_SUB_ROWS) for k in range(x_ref.shape[0] // PLE_SUB_ROWS)]
    hs = [_rmsnorm_rows(x_ref[s, :], g_ref[...]).astype(BF16) for s in subs]
    ts = [jnp.dot(h, wd_ref[...], preferred_element_type=F32).astype(BF16) for h in hs]
    gates = [_sigmoid(jnp.dot(t, wu_ref[...], preferred_element_type=F32)) for t in ts]
    for s, gate in zip(subs, gates):
        proj = jnp.dot(p_ref[s, :].astype(BF16), wp_ref[...], preferred_element_type=F32)
        y = x_ref[s, :] + gate * proj
        o_ref[s, :] = y
        if emit_next:
            hn_ref[s, :] = _rmsnorm_rows(y, gn_ref[...]).astype(hn_ref.dtype)


def _ple(x, p, g, wd, wu, wp, g_next=None, *, tm=256):
    s, d = x.shape
    r = wd.shape[1]
    row = lambda i: (i, 0)
    fixed = lambda i: (0, 0)
    in_specs = [
        pl.BlockSpec((tm, d), row),
        pl.BlockSpec((tm, PLE_DIM), row),
        pl.BlockSpec((1, d), fixed),
        pl.BlockSpec((d, r), fixed),
        pl.BlockSpec((r, d), fixed),
        pl.BlockSpec((PLE_DIM, d), fixed),
    ]
    args = [x, p, g.reshape(1, d), wd, wu, wp]
    out_specs = pl.BlockSpec((tm, d), row)
    out_shape = jax.ShapeDtypeStruct((s, d), F32)
    if g_next is not None:
        in_specs.append(pl.BlockSpec((1, d), fixed))
        args.append(g_next.reshape(1, d))
        out_specs = (out_specs, pl.BlockSpec((tm, d), row))
        out_shape = (out_shape, jax.ShapeDtypeStruct((s, d), BF16))
    return pl.pallas_call(
        _ple_kernel,
        grid=(s // tm,),
        in_specs=in_specs,
        out_specs=out_specs,
        out_shape=out_shape,
        compiler_params=_params("parallel"),
        name="ple",
    )(*args)


def kernel(x, p, norm_mix, w_in, w_pool, pool_scale, q_norm, k_norm, rpb,
           w_branch_pool, w_branch_attn, w_gate, w_out, norm_mlp, w_up, w_down,
           norm_ple, w_ple_gate_down, w_ple_gate_up, w_ple_proj):
    batch, seq, d = x.shape
    assert batch == 1
    depth = w_in.shape[0]
    d_ff = w_up.shape[2]
    xs = x.reshape(seq, d)
    heads_per_block = ATTN_WIDTH // HEAD_DIM
    attn_bias = _attention_bias(rpb)
    h = _rmsnorm(xs, norm_mix[0])
    for i in range(depth):
        qk_gain = jnp.concatenate([jnp.tile(q_norm[i] * QUERY_SCALE, heads_per_block),
                                   jnp.tile(k_norm[i], heads_per_block)]).reshape(1, 2 * ATTN_WIDTH)
        u, qkv = _in_proj(h, w_in, i, qk_gain)
        a = _pool_mixer(u, w_pool[i].astype(BF16), pool_scale[i])
        b = _attention(qkv, attn_bias, i)
        merged = _merge(h, a, b, w_gate, w_branch_pool, w_branch_attn, i)
        xs, hq, sumsq = _weight_matmul(
            _epi_residual_prenorm, merged, w_out, i, n_cols=d, out_dtypes=[F32, BF16], tm=512,
            tiles=[xs], colvecs=[norm_mlp[i].reshape(1, d)], stats=True, name="out_proj")
        up = _weight_matmul(functools.partial(_epi_rowscale_relu2, width=d), hq, w_up, i,
                            n_cols=d_ff, out_dtypes=[BF16], rowblocks=[sumsq], name="mlp_up")
        xs = _weight_matmul(_epi_residual, up, w_down, i, n_cols=d, k_passes=d_ff // d,
                            out_dtypes=[F32], tiles=[xs], name="mlp_down")

        ple_w = (w_ple_gate_down[i].astype(BF16), w_ple_gate_up[i].astype(BF16),
                 w_ple_proj[i].astype(BF16))
        p_i = p[i].reshape(seq, PLE_DIM)
        if i + 1 < depth:
            xs, h = _ple(xs, p_i, norm_ple[i], *ple_w, g_next=norm_mix[i + 1])
        else:
            xs = _ple(xs, p_i, norm_ple[i], *ple_w)
    return xs.reshape(batch, seq, d)
```
